```python
import jax, jax.numpy as jnp
from jax import lax
import numpy as np

D_MODEL = 1024
BATCH = 16
SEQ = 2048
DEPTH = 2
DEC_BATCH = 16
DEC_SEQ = 64
PAST_LEN = 1024

CHUNK = 64
D_POOL = D_MODEL // 2
N_POOL_GROUPS = 4
POOL_GROUP = D_POOL // N_POOL_GROUPS
POOL_WINDOWS = (2, 4, 8, 16)
POOL_BUF = max(POOL_WINDOWS) - 1
D_LSTM = D_MODEL - D_POOL
N_HEADS = 4
HEAD_DIM = D_LSTM // N_HEADS
D_MIX = D_POOL + D_LSTM
D_IN = D_POOL + 4 * D_LSTM + 2 * N_HEADS
D_FF = 2816
CONV_W = 3
CONV_BUF = CONV_W - 1
EPS = 1e-6

kernel_name = 'hymba_pool_mlstm_convffn_stream'


def rmsnorm(x, g):
    xf = x.astype(jnp.float32)
    y = xf * lax.rsqrt(jnp.mean(xf * xf, -1, keepdims=True) + EPS)
    return y * g.astype(jnp.float32)


def pool_mixer(u, buf, start, w_pool, s_pool):
    B, T, _ = u.shape
    uf = u.astype(jnp.float32)
    up = jnp.concatenate([buf.astype(jnp.float32), uf], 1)
    csum = jnp.concatenate([jnp.zeros_like(up[:, :1]), jnp.cumsum(up, 1)], 1)
    end = csum[:, POOL_BUF + 1:]
    pos = start + jnp.arange(T)
    outs = []
    for g, w in enumerate(POOL_WINDOWS):
        sl = slice(g * POOL_GROUP, (g + 1) * POOL_GROUP)
        begin = csum[:, POOL_BUF + 1 - w: POOL_BUF + 1 - w + T, sl]
        cnt = jnp.minimum(w, pos + 1).astype(jnp.float32)[None, :, None]
        outs.append((end[..., sl] - begin) / cnt)
    pooled = jnp.stack(outs, 2)
    diff = pooled - uf.reshape(B, T, N_POOL_GROUPS, POOL_GROUP)
    y = jnp.einsum('btgc,gcd->btgd', diff, w_pool.astype(jnp.float32)).reshape(B, T, D_POOL)
    return y * s_pool.astype(jnp.float32), up[:, -POOL_BUF:]


def mlstm_chunk(carry, xs):
    C, n, m = carry
    q, k, v, ig, lf = xs
    L = q.shape[2]
    b = jnp.cumsum(lf, -1)
    dmat = b[..., :, None] - b[..., None, :] + ig[..., None, :]
    causal = jnp.tril(jnp.ones((L, L), dtype=bool))
    dmat = jnp.where(causal, dmat, -jnp.inf)
    inter = b + m[..., None]
    m_t = jnp.maximum(inter, dmat.max(-1))
    wt = jnp.exp(dmat - m_t[..., None])
    g_inter = jnp.exp(inter - m_t)
    a = wt * jnp.einsum('bhtd,bhsd->bhts', q, k)
    num = jnp.einsum('bhts,bhsd->bhtd', a, v) + g_inter[..., None] * jnp.einsum('bhvk,bhtk->bhtv', C, q)
    den = a.sum(-1) + g_inter * jnp.einsum('bhk,bhtk->bht', n, q)
    h = num / jnp.maximum(jnp.abs(den), jnp.exp(-m_t))[..., None]
    w_last = wt[..., -1, :]
    g_last = g_inter[..., -1]
    C_new = g_last[..., None, None] * C + jnp.einsum('bhs,bhsv,bhsk->bhvk', w_last, v, k)
    n_new = g_last[..., None] * n + jnp.einsum('bhs,bhsk->bhk', w_last, k)
    return (C_new, n_new, m_t[..., -1]), h


def mlstm_seq(q, k, v, ig, lf, C, n, m):
    B, T, H, Dh = q.shape
    L = CHUNK if T % CHUNK == 0 else T
    nc = T // L

    def to_chunks(a):
        a = a.astype(jnp.float32).reshape((B, nc, L) + a.shape[2:])
        return jnp.moveaxis(jnp.moveaxis(a, 1, 0), 2, 3)

    xs = (to_chunks(q), to_chunks(k), to_chunks(v), to_chunks(ig), to_chunks(lf))
    carry = (C.astype(jnp.float32), n.astype(jnp.float32), m.astype(jnp.float32))
    (C, n, m), h = lax.scan(mlstm_chunk, carry, xs)
    h = jnp.moveaxis(jnp.moveaxis(h, 3, 2), 0, 1).reshape(B, T, H, Dh)
    return h, C, n, m


def layer(x, c_act, pool_buf, C, n, m, conv_buf, start,
          w_ada, b_ada, g1, w_in, b_gate, w_pool, s_pool, g_head, w_out,
          g2, w_up, w_conv, b_conv, w_down):
    B, T, _ = x.shape
    mod = (c_act @ w_ada + b_ada)[:, None, :]
    sh1, sc1, gt1, sh2, sc2, gt2 = jnp.split(mod, 6, -1)
    h = rmsnorm(x, g1) * (1 + sc1) + sh1
    proj = h @ w_in
    cuts = [D_POOL, D_POOL + D_LSTM, D_POOL + 2 * D_LSTM, D_POOL + 3 * D_LSTM, D_POOL + 4 * D_LSTM]
    u, q, k, v, o, gates = jnp.split(proj, cuts, -1)
    gates = gates.astype(jnp.float32) + b_gate.astype(jnp.float32)
    ig = gates[..., :N_HEADS]
    lf = jax.nn.log_sigmoid(gates[..., N_HEADS:])
    pool_out, pool_buf_new = pool_mixer(u, pool_buf, start, w_pool, s_pool)
    hd = (B, T, N_HEADS, HEAD_DIM)
    hs, C, n, m = mlstm_seq(q.reshape(hd), k.reshape(hd) * (HEAD_DIM ** -0.5), v.reshape(hd), ig, lf, C, n, m)
    hs = hs * lax.rsqrt(jnp.mean(hs * hs, -1, keepdims=True) + EPS) * g_head.reshape(N_HEADS, HEAD_DIM)
    hs = hs.reshape(B, T, D_LSTM) * jax.nn.sigmoid(o.astype(jnp.float32))
    mix = jnp.concatenate([pool_out, hs], -1) @ w_out
    x = x + gt1 * mix
    h2 = rmsnorm(x, g2) * (1 + sc2) + sh2
    up = h2 @ w_up
    upp = jnp.concatenate([conv_buf.astype(up.dtype), up], 1)
    conv = b_conv + sum(w_conv[j] * upp[:, j:j + T] for j in range(CONV_W))
    a, g = jnp.split(conv, 2, -1)
    x = x + gt2 * ((jax.nn.silu(g) * a) @ w_down)
    return x, pool_buf_new, C, n, m, upp[:, -CONV_BUF:]


def setup_inputs(seed: int = 0) -> dict:
    key = jax.random.key(seed)
    ks = jax.random.split(key, 32)
    f32 = jnp.float32
    nrm = lambda k, s, sc: jax.random.normal(k, s, f32) * sc
    b_i = nrm(ks[0], (DEPTH, N_HEADS), 0.1)
    b_f = jnp.linspace(3.0, 6.0, N_HEADS, dtype=f32)[None, :] + nrm(ks[1], (DEPTH, N_HEADS), 0.1)
    return {
        'x_prompt': nrm(ks[2], (BATCH, SEQ, D_MODEL), 1.0),
        'x_sample': nrm(ks[3], (DEC_BATCH, DEC_SEQ, D_MODEL), 1.0),
        'state_pool': nrm(ks[4], (DEPTH, DEC_BATCH, POOL_BUF, D_POOL), 1.0),
        'state_mlstm_C': nrm(ks[5], (DEPTH, DEC_BATCH, N_HEADS, HEAD_DIM, HEAD_DIM), 0.1),
        'state_mlstm_n': nrm(ks[6], (DEPTH, DEC_BATCH, N_HEADS, HEAD_DIM), 0.1),
        'state_mlstm_m': jax.random.uniform(ks[7], (DEPTH, DEC_BATCH, N_HEADS), f32, 0.0, 1.0),
        'state_conv': nrm(ks[8], (DEPTH, DEC_BATCH, CONV_BUF, 2 * D_FF), 1.0),
        'c_prompt': nrm(ks[9], (BATCH, D_MODEL), 1.0),
        'c_sample': nrm(ks[10], (DEC_BATCH, D_MODEL), 1.0),
        'w_ada': nrm(ks[11], (DEPTH, D_MODEL, 6 * D_MODEL), 0.5 * D_MODEL ** -0.5),
        'b_ada': nrm(ks[12], (DEPTH, 6 * D_MODEL), 0.01),
        'g_norm1': 1.0 + nrm(ks[13], (DEPTH, D_MODEL), 0.05),
        'w_in': nrm(ks[14], (DEPTH, D_MODEL, D_IN), D_MODEL ** -0.5),
        'b_gate': jnp.concatenate([b_i, b_f], -1),
        'w_pool': nrm(ks[15], (DEPTH, N_POOL_GROUPS, POOL_GROUP, POOL_GROUP), POOL_GROUP ** -0.5),
        's_pool': 1.0 + nrm(ks[16], (DEPTH, D_POOL), 0.05),
        'g_head': 1.0 + nrm(ks[17], (DEPTH, D_LSTM), 0.05),
        'w_out': nrm(ks[18], (DEPTH, D_MIX, D_MODEL), D_MIX ** -0.5),
        'g_norm2': 1.0 + nrm(ks[19], (DEPTH, D_MODEL), 0.05),
        'w_up': nrm(ks[20], (DEPTH, D_MODEL, 2 * D_FF), D_MODEL ** -0.5),
        'w_conv': nrm(ks[21], (DEPTH, CONV_W, 2 * D_FF), CONV_W ** -0.5),
        'b_conv': nrm(ks[22], (DEPTH, 2 * D_FF), 0.01),
        'w_down': nrm(ks[23], (DEPTH, D_FF, D_MODEL), D_FF ** -0.5),
        'g_final': 1.0 + nrm(ks[24], (D_MODEL,), 0.05),
    }


def reference(x_prompt, x_sample, state_pool, state_mlstm_C, state_mlstm_n, state_mlstm_m, state_conv,
              c_prompt, c_sample, w_ada, b_ada, g_norm1, w_in, b_gate, w_pool, s_pool, g_head, w_out,
              g_norm2, w_up, w_conv, b_conv, w_down, g_final):
    f32 = jnp.float32
    cp = jax.nn.silu(c_prompt)
    cs = jax.nn.silu(c_sample)
    xp, xs = x_prompt, x_sample
    Bp = x_prompt.shape[0]
    pp, Cp, npl, mp, vp = [], [], [], [], []
    ps, Cs, nsl, ms, vs = [], [], [], [], []
    for l in range(DEPTH):
        prm = (w_ada[l], b_ada[l], g_norm1[l], w_in[l], b_gate[l], w_pool[l], s_pool[l], g_head[l],
               w_out[l], g_norm2[l], w_up[l], w_conv[l], b_conv[l], w_down[l])
        xp, a, b, c, d, e = layer(
            xp, cp,
            jnp.zeros((Bp, POOL_BUF, D_POOL), f32),
            jnp.zeros((Bp, N_HEADS, HEAD_DIM, HEAD_DIM), f32),
            jnp.zeros((Bp, N_HEADS, HEAD_DIM), f32),
            jnp.zeros((Bp, N_HEADS), f32),
            jnp.zeros((Bp, CONV_BUF, 2 * D_FF), f32),
            0, *prm)
        pp.append(a); Cp.append(b); npl.append(c); mp.append(d); vp.append(e)
        xs, a, b, c, d, e = layer(
            xs, cs, state_pool[l], state_mlstm_C[l], state_mlstm_n[l], state_mlstm_m[l], state_conv[l],
            PAST_LEN, *prm)
        ps.append(a); Cs.append(b); nsl.append(c); ms.append(d); vs.append(e)
    y_prompt = rmsnorm(xp, g_final)
    y_sample = rmsnorm(xs, g_final)
    return (y_prompt, y_sample,
            jnp.stack(pp), jnp.stack(Cp), jnp.stack(npl), jnp.stack(mp), jnp.stack(vp),
            jnp.stack(ps), jnp.stack(Cs), jnp.stack(nsl), jnp.stack(ms), jnp.stack(vs))
```

```python
import functools

import jax
import jax.numpy as jnp
from jax import lax
from jax.experimental import pallas as pl
from jax.experimental.pallas import tpu as pltpu

D_MODEL = 1024
D_POOL = 512
N_POOL_GROUPS = 4
POOL_GROUP = 128
POOL_WINDOWS = (2, 4, 8, 16)
POOL_BUF = 15
D_LSTM = 512
N_HEADS = 4
HEAD_DIM = 128
D_MAIN = D_POOL + 4 * D_LSTM
D_FF = 2816
CONV_W = 3
CONV_BUF = 2
EPS = 1e-6
PAST_LEN = 1024

LANES = 128
SUBLANES = 8
MXU_DIM = 256
VMEM_LIMIT_BYTES = 56 * 1024 * 1024

D_INP = D_MAIN + LANES
GATE_F_OFF = SUBLANES
POOL_HIST = 2 * SUBLANES
CONV_HIST = SUBLANES
FF_CHUNK = MXU_DIM
N_FF_CHUNKS = D_FF // FF_CHUNK
STATE_ROWS = 2 * HEAD_DIM
NEG_BIG = -1e30

_NT = (((1,), (1,)), ((), ()))
_TN = (((0,), (0,)), ((), ()))
F32 = jnp.float32
BF16 = jnp.bfloat16


def _rms(v):
    return v * lax.rsqrt(jnp.mean(v * v, axis=-1, keepdims=True) + EPS)


def _sigmoid(v):
    return 1.0 / (1.0 + jnp.exp(-v))


def _dot(a, b):
    return jnp.dot(a, b, preferred_element_type=F32)


def _mod_kernel(c_ref, w_ref, b_ref, o_ref):
    c = c_ref[...]
    act = (c * _sigmoid(c)).astype(BF16)
    o_ref[0] = _dot(act, w_ref[0].astype(BF16)) + b_ref[0]


def _adaln_mod(c_all, w_ada, b_ada):
    depth = w_ada.shape[0]
    rows = c_all.shape[0]
    n_out = w_ada.shape[2]
    tn = 1536
    return pl.pallas_call(
        _mod_kernel,
        grid=(depth, n_out // tn),
        in_specs=[
            pl.BlockSpec((rows, D_MODEL), lambda l, j: (0, 0)),
            pl.BlockSpec((1, D_MODEL, tn), lambda l, j: (l, 0, j)),
            pl.BlockSpec((1, 1, tn), lambda l, j: (l, 0, j)),
        ],
        out_specs=pl.BlockSpec((1, rows, tn), lambda l, j: (l, 0, j)),
        out_shape=jax.ShapeDtypeStruct((depth, rows, n_out), F32),
        compiler_params=pltpu.CompilerParams(
            dimension_semantics=("arbitrary", "arbitrary"),
            vmem_limit_bytes=VMEM_LIMIT_BYTES),
        name="adaln_mod",
    )(c_all, w_ada, b_ada.reshape(depth, 1, n_out))


def _layer_kernel(x_ref, mod_ref, pool0_ref, st0_ref, m0_ref, conv0_ref,
                  g1_ref, win_ref, bg_ref, wpool_ref, spool_ref, ghead_ref, wout_ref,
                  g2_ref, wup_ref, wcv_ref, wdown_ref, gfin_ref,
                  y_ref, poolo_ref, sto_ref, mo_ref, convo_ref,
                  hist_ref, st_ref, m_ref, cbuf_ref, proj_ref, acc_ref,
                  *, tt, valid, start, final):
    t = pl.program_id(1)

    @pl.when(t == 0)
    def _load_state():
        hist_ref[...] = pool0_ref[0]
        st_ref[...] = st0_ref[0]
        m_ref[...] = m0_ref[0]
        cbuf_ref[...] = conv0_ref[0]

    x = x_ref[0]
    mod = mod_ref[0]
    sh1, sc1, gt1 = mod[0:1], mod[1:2], mod[2:3]
    sh2, sc2, gt2 = mod[3:4], mod[4:5], mod[5:6]

    h = _rms(x) * (g1_ref[...] * (1.0 + sc1)) + sh1
    proj_ref[...] = _dot(h.astype(BF16), win_ref[...])

    pos = start + t * tt + lax.broadcasted_iota(jnp.int32, (tt, 1), 0)
    u = proj_ref[:, 0:D_POOL]
    ext = jnp.concatenate([hist_ref[...], u], axis=0)
    pool_parts = []
    for g, w in enumerate(POOL_WINDOWS):
        sl = slice(g * POOL_GROUP, (g + 1) * POOL_GROUP)
        c = ext[:, sl]
        shift = 1
        while shift < w:
            c = c + pltpu.roll(c, shift, axis=0)
            shift *= 2
        cnt = jnp.minimum(w, pos + 1).astype(F32)
        diff = c[POOL_HIST:] * (1.0 / cnt) - u[:, sl]
        pool_parts.append(_dot(diff.astype(BF16), wpool_ref[g]))
    pool_out = jnp.concatenate(pool_parts, axis=1) * spool_ref[...]
    hist_ref[...] = u[valid - POOL_HIST:valid]

    gates_t = (proj_ref[:, D_MAIN:D_INP] + bg_ref[...]).T
    ig = gates_t[0:SUBLANES]
    fg = gates_t[GATE_F_OFF:GATE_F_OFF + SUBLANES]
    lf = jnp.minimum(fg, 0.0) - jnp.log1p(jnp.exp(-jnp.abs(fg)))
    lane = lax.broadcasted_iota(jnp.int32, (SUBLANES, tt), 1)
    b = lf
    shift = 1
    while shift < tt:
        b = b + jnp.where(lane >= shift, pltpu.roll(b, shift, axis=1), 0.0)
        shift *= 2
    d = ig - b
    gmax = d
    shift = 1
    while shift < tt:
        gmax = jnp.maximum(gmax, jnp.where(lane >= shift, pltpu.roll(gmax, shift, axis=1), NEG_BIG))
        shift *= 2
    m_prev = m_ref[:, 0:1]
    m_t = b + jnp.maximum(m_prev, gmax)
    col_a = b - m_t
    g_inter = jnp.exp(b + m_prev - m_t)
    e_negm = jnp.exp(-m_t)
    last = valid - 1
    w_last = jnp.where(lane < valid, jnp.exp(col_a[:, last:last + 1] + d), 0.0)
    g_last = g_inter[:, last:last + 1]
    m_ref[...] = jnp.broadcast_to(m_t[:, last:last + 1], (SUBLANES, LANES))
    stack = jnp.concatenate(
        [col_a, g_inter, e_negm, w_last, jnp.zeros((LANES - 4 * SUBLANES, tt), F32)], axis=0)
    cols = stack.T
    causal = (lax.broadcasted_iota(jnp.int32, (tt, tt), 0)
              >= lax.broadcasted_iota(jnp.int32, (tt, tt), 1))

    head_parts = []
    for hd in range(N_HEADS):
        off = hd * HEAD_DIM
        q = proj_ref[:, D_POOL + off:D_POOL + off + HEAD_DIM]
        k = proj_ref[:, D_POOL + D_LSTM + off:D_POOL + D_LSTM + off + HEAD_DIM] * (HEAD_DIM ** -0.5)
        v = proj_ref[:, D_POOL + 2 * D_LSTM + off:D_POOL + 2 * D_LSTM + off + HEAD_DIM]
        o = proj_ref[:, D_POOL + 3 * D_LSTM + off:D_POOL + 3 * D_LSTM + off + HEAD_DIM]
        qb, vb = q.astype(BF16), v.astype(BF16)
        scores = lax.dot_general(qb, k.astype(BF16), _NT, preferred_element_type=F32)
        arg = cols[:, hd:hd + 1] + d[hd:hd + 1, :]
        a = jnp.exp(jnp.where(causal, arg, NEG_BIG)) * scores
        state = st_ref[hd]
        inter = lax.dot_general(qb, state.astype(BF16), _NT, preferred_element_type=F32)
        gi = cols[:, SUBLANES + hd:SUBLANES + hd + 1]
        num = _dot(a.astype(BF16), vb) + gi * inter[:, 0:HEAD_DIM]
        den = jnp.sum(a, axis=-1, keepdims=True) + gi * inter[:, HEAD_DIM:HEAD_DIM + 1]
        floor = cols[:, 2 * SUBLANES + hd:2 * SUBLANES + hd + 1]
        hh = num * (1.0 / jnp.maximum(jnp.abs(den), floor))
        hn = _rms(hh) * ghead_ref[:, off:off + HEAD_DIM]
        head_parts.append(hn * _sigmoid(o))
        wk = cols[:, 3 * SUBLANES + hd:3 * SUBLANES + hd + 1] * k
        upd = lax.dot_general(vb, wk.astype(BF16), _TN, preferred_element_type=F32)
        gl = g_last[hd:hd + 1, :]
        st_ref[hd, 0:HEAD_DIM, :] = gl * state[0:HEAD_DIM] + upd
        st_ref[hd, HEAD_DIM:HEAD_DIM + 1, :] = (
            gl * state[HEAD_DIM:HEAD_DIM + 1] + jnp.sum(wk, axis=0, keepdims=True))

    mix_in = jnp.concatenate([pool_out] + head_parts, axis=1).astype(BF16)
    x1 = x + gt1 * _dot(mix_in, wout_ref[...])

    h2 = (_rms(x1) * (g2_ref[...] * (1.0 + sc2)) + sh2).astype(BF16)
    acc_ref[...] = jnp.zeros_like(acc_ref)

    def conv_chunk(c):
        up = _dot(h2, wup_ref[c])
        ue = jnp.concatenate([cbuf_ref[c], up], axis=0)
        wc = wcv_ref[c]
        cbuf_ref[c] = up[valid - CONV_HIST:valid]
        return (wc[3:4] + wc[0:1] * pltpu.roll(ue, 2, axis=0)[CONV_HIST:]
                + wc[1:2] * pltpu.roll(ue, 1, axis=0)[CONV_HIST:] + wc[2:3] * up)

    def ff_body(j, carry):
        val = conv_chunk(j)
        gate = conv_chunk(j + N_FF_CHUNKS)
        act = (gate * _sigmoid(gate)) * val
        acc_ref[...] += _dot(act.astype(BF16), wdown_ref[j])
        return carry

    lax.fori_loop(0, N_FF_CHUNKS, ff_body, 0)
    x2 = x1 + gt2 * acc_ref[...]
    if final:
        x2 = _rms(x2) * gfin_ref[...]
    y_ref[0] = x2

    @pl.when(t == pl.num_programs(1) - 1)
    def _store_state():
        poolo_ref[0] = hist_ref[...]
        sto_ref[0] = st_ref[...]
        mo_ref[0] = m_ref[...]
        convo_ref[0] = cbuf_ref[...]


def _resident(shape):
    nd = len(shape)
    return pl.BlockSpec(shape, lambda b, t: (0,) * nd, pipeline_mode=pl.Buffered(1))


def _run_layer(x, mod, pool0, st0, m0, conv0, prm, g_final, *, tt, valid, start, final):
    bsz, t_len, _ = x.shape
    nt = t_len // tt
    n_cc = 2 * N_FF_CHUNKS
    per_b3 = lambda b, t: (b, 0, 0)
    per_b4 = lambda b, t: (b, 0, 0, 0)
    in_specs = [
        pl.BlockSpec((1, tt, D_MODEL), lambda b, t: (b, t, 0)),
        pl.BlockSpec((1, 6, D_MODEL), per_b3),
        pl.BlockSpec((1, POOL_HIST, D_POOL), per_b3),
        pl.BlockSpec((1, N_HEADS, STATE_ROWS, HEAD_DIM), per_b4),
        pl.BlockSpec((1, SUBLANES, LANES), per_b3),
        pl.BlockSpec((1, n_cc, CONV_HIST, FF_CHUNK), per_b4),
        _resident((1, D_MODEL)),
        _resident((D_MODEL, D_INP)),
        _resident((1, LANES)),
        _resident((N_POOL_GROUPS, POOL_GROUP, POOL_GROUP)),
        _resident((1, D_POOL)),
        _resident((1, D_LSTM)),
        _resident((D_MODEL, D_MODEL)),
        _resident((1, D_MODEL)),
        _resident((n_cc, D_MODEL, FF_CHUNK)),
        _resident((n_cc, SUBLANES, FF_CHUNK)),
        _resident((N_FF_CHUNKS, FF_CHUNK, D_MODEL)),
        _resident((1, D_MODEL)),
    ]
    out_specs = [
        pl.BlockSpec((1, tt, D_MODEL), lambda b, t: (b, t, 0)),
        pl.BlockSpec((1, POOL_HIST, D_POOL), per_b3),
        pl.BlockSpec((1, N_HEADS, STATE_ROWS, HEAD_DIM), per_b4),
        pl.BlockSpec((1, SUBLANES, LANES), per_b3),
        pl.BlockSpec((1, n_cc, CONV_HIST, FF_CHUNK), per_b4),
    ]
    out_shape = [
        jax.ShapeDtypeStruct((bsz, t_len, D_MODEL), F32),
        jax.ShapeDtypeStruct((bsz, POOL_HIST, D_POOL), F32),
        jax.ShapeDtypeStruct((bsz, N_HEADS, STATE_ROWS, HEAD_DIM), F32),
        jax.ShapeDtypeStruct((bsz, SUBLANES, LANES), F32),
        jax.ShapeDtypeStruct((bsz, n_cc, CONV_HIST, FF_CHUNK), F32),
    ]
    scratch = [
        pltpu.VMEM((POOL_HIST, D_POOL), F32),
        pltpu.VMEM((N_HEADS, STATE_ROWS, HEAD_DIM), F32),
        pltpu.VMEM((SUBLANES, LANES), F32),
        pltpu.VMEM((n_cc, CONV_HIST, FF_CHUNK), F32),
        pltpu.VMEM((tt, D_INP), F32),
        pltpu.VMEM((tt, D_MODEL), F32),
    ]
    kern = functools.partial(_layer_kernel, tt=tt, valid=valid, start=start, final=final)
    return pl.pallas_call(
        kern,
        grid=(bsz, nt),
        in_specs=in_specs,
        out_specs=out_specs,
        out_shape=out_shape,
        scratch_shapes=scratch,
        compiler_params=pltpu.CompilerParams(
            dimension_semantics=("arbitrary", "arbitrary"),
            vmem_limit_bytes=VMEM_LIMIT_BYTES),
        name="layer_t%d" % tt,
    )(x, mod, pool0, st0, m0, conv0, *prm, g_final)


def _prep_layer_params(g1, w_in, b_gate, w_pool, s_pool, g_head, w_out, g2, w_up, w_conv, b_conv, w_down):
    n_cc = 2 * N_FF_CHUNKS
    gate_w = jnp.zeros((D_MODEL, LANES), F32)
    gate_w = gate_w.at[:, 0:N_HEADS].set(w_in[:, D_MAIN:D_MAIN + N_HEADS])
    gate_w = gate_w.at[:, GATE_F_OFF:GATE_F_OFF + N_HEADS].set(w_in[:, D_MAIN + N_HEADS:])
    w_in_p = jnp.concatenate([w_in[:, :D_MAIN], gate_w], axis=1).astype(BF16)
    bg = jnp.zeros((1, LANES), F32)
    bg = bg.at[0, 0:N_HEADS].set(b_gate[:N_HEADS])
    bg = bg.at[0, GATE_F_OFF:GATE_F_OFF + N_HEADS].set(b_gate[N_HEADS:])
    w_up_c = w_up.reshape(D_MODEL, n_cc, FF_CHUNK).transpose(1, 0, 2).astype(BF16)
    taps = jnp.concatenate([w_conv, b_conv[None, :], jnp.zeros((SUBLANES - CONV_W - 1, 2 * D_FF), F32)], axis=0)
    w_cv_c = taps.reshape(SUBLANES, n_cc, FF_CHUNK).transpose(1, 0, 2)
    w_down_c = w_down.reshape(N_FF_CHUNKS, FF_CHUNK, D_MODEL).astype(BF16)
    return (g1.reshape(1, D_MODEL), w_in_p, bg, w_pool.astype(BF16), s_pool.reshape(1, D_POOL),
            g_head.reshape(1, D_LSTM), w_out.astype(BF16), g2.reshape(1, D_MODEL),
            w_up_c, w_cv_c, w_down_c)


def _pack_states(pool, c_mat, n_vec, m_vec, conv):
    bsz = pool.shape[0]
    n_cc = 2 * N_FF_CHUNKS
    pool_p = jnp.pad(pool, ((0, 0), (POOL_HIST - POOL_BUF, 0), (0, 0)))
    st = jnp.concatenate(
        [c_mat, n_vec[:, :, None, :], jnp.zeros((bsz, N_HEADS, STATE_ROWS - HEAD_DIM - 1, HEAD_DIM), F32)], axis=2)
    m_p = jnp.broadcast_to(
        jnp.pad(m_vec, ((0, 0), (0, SUBLANES - N_HEADS)))[:, :, None], (bsz, SUBLANES, LANES))
    conv_p = jnp.pad(conv, ((0, 0), (CONV_HIST - CONV_BUF, 0), (0, 0)))
    conv_p = conv_p.reshape(bsz, CONV_HIST, n_cc, FF_CHUNK).transpose(0, 2, 1, 3)
    return pool_p, st, m_p, conv_p


def _unpack_states(pool_p, st, m_p, conv_p):
    bsz = pool_p.shape[0]
    conv = conv_p.transpose(0, 2, 1, 3).reshape(bsz, CONV_HIST, 2 * D_FF)
    return (pool_p[:, POOL_HIST - POOL_BUF:], st[:, :, :HEAD_DIM], st[:, :, HEAD_DIM],
            m_p[:, :N_HEADS, 0], conv[:, CONV_HIST - CONV_BUF:])


def kernel(x_prompt, x_sample, state_pool, state_mlstm_C, state_mlstm_n, state_mlstm_m, state_conv, c_prompt, c_sample, w_ada, b_ada, g_norm1, w_in, b_gate, w_pool, s_pool, g_head, w_out, g_norm2, w_up, w_conv, b_conv, w_down, g_final):
    depth = w_ada.shape[0]
    bp, seq, _ = x_prompt.shape
    bs, dec_seq, _ = x_sample.shape
    tt_p = 256
    tt_s = LANES
    assert seq % tt_p == 0 and POOL_HIST <= dec_seq <= tt_s and dec_seq % SUBLANES == 0

    mod = _adaln_mod(jnp.concatenate([c_prompt, c_sample], axis=0), w_ada, b_ada)
    mod = mod.reshape(depth, bp + bs, 6, D_MODEL)
    gfin = g_final.reshape(1, D_MODEL)

    xp = x_prompt
    xs = jnp.pad(x_sample, ((0, 0), (0, tt_s - dec_seq), (0, 0)))
    zero_states = _pack_states(
        jnp.zeros((bp, POOL_BUF, D_POOL), F32), jnp.zeros((bp, N_HEADS, HEAD_DIM, HEAD_DIM), F32),
        jnp.zeros((bp, N_HEADS, HEAD_DIM), F32), jnp.zeros((bp, N_HEADS), F32),
        jnp.zeros((bp, CONV_BUF, 2 * D_FF), F32))
    outs_p, outs_s = [], []
    for l in range(depth):
        prm = _prep_layer_params(g_norm1[l], w_in[l], b_gate[l], w_pool[l], s_pool[l], g_head[l],
                                 w_out[l], g_norm2[l], w_up[l], w_conv[l], b_conv[l], w_down[l])
        final = l == depth - 1
        res = _run_layer(xp, mod[l, :bp], *zero_states, prm, gfin,
                         tt=tt_p, valid=tt_p, start=0, final=final)
        xp = res[0]
        outs_p.append(_unpack_states(*res[1:]))
        sample_states = _pack_states(state_pool[l], state_mlstm_C[l], state_mlstm_n[l],
                                     state_mlstm_m[l], state_conv[l])
        res = _run_layer(xs, mod[l, bp:], *sample_states, prm, gfin,
                         tt=tt_s, valid=dec_seq, start=PAST_LEN, final=final)
        xs = res[0]
        outs_s.append(_unpack_states(*res[1:]))
    stack = lambda outs, i: jnp.stack([o[i] for o in outs])
    return ((xp, xs[:, :dec_seq])
            + tuple(stack(outs_p, i) for i in range(5))
            + tuple(stack(outs_s, i) for i in range(5)))
```

```python
import functools

import jax
import jax.numpy as jnp
from jax import lax
from jax.experimental import pallas as pl
from jax.experimental.pallas import tpu as pltpu

D_MODEL = 1024
D_POOL = 512
N_POOL_GROUPS = 4
POOL_GROUP = 128
POOL_WINDOWS = (2, 4, 8, 16)
POOL_BUF = 15
D_LSTM = 512
N_HEADS = 4
HEAD_DIM = 128
D_MAIN = D_POOL + 4 * D_LSTM
D_FF = 2816
CONV_W = 3
CONV_BUF = 2
EPS = 1e-6
PAST_LEN = 1024

LANES = 128
SUBLANES = 8
MXU_DIM = 256
VMEM_LIMIT_BYTES = 56 * 1024 * 1024

D_INP = D_MAIN + 2 * LANES
POOL_HIST = 2 * SUBLANES
CONV_HIST = SUBLANES
FF_CHUNK = MXU_DIM
N_FF_CHUNKS = D_FF // FF_CHUNK
STATE_ROWS = 2 * HEAD_DIM
NEG_BIG = -1e30

_NT = (((1,), (1,)), ((), ()))
_TN = (((0,), (0,)), ((), ()))
F32 = jnp.float32
BF16 = jnp.bfloat16


def _rms(v):
    return v * lax.rsqrt(jnp.mean(v * v, axis=-1, keepdims=True) + EPS)


def _sigmoid(v):
    return 1.0 / (1.0 + jnp.exp(-v))


def _dot(a, b):
    return jnp.dot(a, b, preferred_element_type=F32)


def _mod_kernel(c_ref, w_ref, b_ref, o_ref):
    c = c_ref[...]
    act = (c * _sigmoid(c)).astype(BF16)
    o_ref[0] = _dot(act, w_ref[0].astype(BF16)) + b_ref[0]


def _adaln_mod(c_all, w_ada, b_ada):
    depth = w_ada.shape[0]
    rows = c_all.shape[0]
    n_out = w_ada.shape[2]
    tn = 1536
    return pl.pallas_call(
        _mod_kernel,
        grid=(depth, n_out // tn),
        in_specs=[
            pl.BlockSpec((rows, D_MODEL), lambda l, j: (0, 0)),
            pl.BlockSpec((1, D_MODEL, tn), lambda l, j: (l, 0, j)),
            pl.BlockSpec((1, 1, tn), lambda l, j: (l, 0, j)),
        ],
        out_specs=pl.BlockSpec((1, rows, tn), lambda l, j: (l, 0, j)),
        out_shape=jax.ShapeDtypeStruct((depth, rows, n_out), F32),
        compiler_params=pltpu.CompilerParams(
            dimension_semantics=("arbitrary", "arbitrary"),
            vmem_limit_bytes=VMEM_LIMIT_BYTES),
        name="adaln_mod",
    )(c_all, w_ada, b_ada.reshape(depth, 1, n_out))


def _layer_kernel(x_ref, mod_ref, pool0_ref, st0_ref, m0_ref, conv0_ref,
                  g1_ref, win_ref, bg_ref, wpool_ref, spool_ref, ghead_ref, wout_ref,
                  g2_ref, wup_ref, wcv_ref, wdown_ref, gfin_ref,
                  y_ref, poolo_ref, sto_ref, mo_ref, convo_ref,
                  hist_ref, st_ref, m_ref, cbuf_ref, proj_ref, acc_ref,
                  *, tt, valid, start, final):
    t = pl.program_id(1)

    @pl.when(t == 0)
    def _load_state():
        hist_ref[...] = pool0_ref[0]
        st_ref[...] = st0_ref[0]
        m_ref[...] = m0_ref[0]
        cbuf_ref[...] = conv0_ref[0]

    x = x_ref[0]
    mod = mod_ref[0]
    sh1, sc1, gt1 = mod[0:1], mod[1:2], mod[2:3]
    sh2, sc2, gt2 = mod[3:4], mod[4:5], mod[5:6]

    h = _rms(x) * (g1_ref[...] * (1.0 + sc1)) + sh1
    proj_ref[...] = _dot(h.astype(BF16), win_ref[...])

    rows_tt = lax.broadcasted_iota(jnp.int32, (tt, tt), 0)
    causal = rows_tt >= lax.broadcasted_iota(jnp.int32, (tt, tt), 1)
    ig = proj_ref[:, D_MAIN:D_MAIN + LANES] + bg_ref[:, 0:LANES]
    fg = proj_ref[:, D_MAIN + LANES:D_INP] + bg_ref[:, LANES:2 * LANES]
    lf = jnp.minimum(fg, 0.0) - jnp.log1p(jnp.exp(-jnp.abs(fg)))
    lf_hi = lf.astype(BF16)
    rem = lf - lf_hi.astype(F32)
    lf_mid = rem.astype(BF16)
    lf_lo = (rem - lf_mid.astype(F32)).astype(BF16)
    tri = jnp.where(causal, 1.0, 0.0).astype(BF16)
    b2 = _dot(tri, jnp.concatenate([lf_hi, lf_mid], axis=1))
    b = b2[:, 0:LANES] + b2[:, LANES:2 * LANES] + _dot(tri, lf_lo)
    d = ig - b
    d_rows = d.T

    pos = start + t * tt + lax.broadcasted_iota(jnp.int32, (tt, 1), 0)
    u = proj_ref[:, 0:D_POOL]
    ext = jnp.concatenate([hist_ref[...], u], axis=0)
    pool_parts = []
    for g, w in enumerate(POOL_WINDOWS):
        sl = slice(g * POOL_GROUP, (g + 1) * POOL_GROUP)
        c = ext[:, sl]
        shift = 1
        while shift < w:
            c = c + pltpu.roll(c, shift, axis=0)
            shift *= 2
        cnt = jnp.minimum(w, pos + 1).astype(F32)
        diff = c[POOL_HIST:] * (1.0 / cnt) - u[:, sl]
        pool_parts.append(_dot(diff.astype(BF16), wpool_ref[g]))
    pool_out = jnp.concatenate(pool_parts, axis=1) * spool_ref[...]
    hist_ref[...] = u[valid - POOL_HIST:valid]

    ks, scores, inters, states = [], [], [], []
    for hd in range(N_HEADS):
        off = hd * HEAD_DIM
        q = proj_ref[:, D_POOL + off:D_POOL + off + HEAD_DIM].astype(BF16)
        k = proj_ref[:, D_POOL + D_LSTM + off:D_POOL + D_LSTM + off + HEAD_DIM] * (HEAD_DIM ** -0.5)
        state = st_ref[hd]
        scores.append(lax.dot_general(q, k.astype(BF16), _NT, preferred_element_type=F32))
        inters.append(lax.dot_general(q, state.astype(BF16), _NT, preferred_element_type=F32))
        ks.append(k)
        states.append(state)

    lane_id = lax.broadcasted_iota(jnp.int32, (tt, LANES), 1)
    masked_d, gmax = [], jnp.full((tt, LANES), NEG_BIG, F32)
    for hd in range(N_HEADS):
        md = jnp.where(causal, d_rows[hd:hd + 1, :], NEG_BIG)
        masked_d.append(md)
        gmax = jnp.where(lane_id == hd, jnp.max(md, axis=-1, keepdims=True), gmax)
    m_prev = m_ref[0:1, :]
    m_t = b + jnp.maximum(m_prev, gmax)
    col_a = b - m_t
    g_inter = jnp.exp(b + m_prev - m_t)
    e_negm = jnp.exp(-m_t)
    last = valid - 1
    row_id = lax.broadcasted_iota(jnp.int32, (tt, LANES), 0)
    w_last = jnp.where(row_id < valid, jnp.exp(col_a[last:last + 1, :] + d), 0.0)
    g_last = g_inter[last:last + 1, :]
    m_ref[...] = jnp.broadcast_to(m_t[last:last + 1, :], (SUBLANES, LANES))

    head_parts = []
    for hd in range(N_HEADS):
        off = hd * HEAD_DIM
        v = proj_ref[:, D_POOL + 2 * D_LSTM + off:D_POOL + 2 * D_LSTM + off + HEAD_DIM].astype(BF16)
        o = proj_ref[:, D_POOL + 3 * D_LSTM + off:D_POOL + 3 * D_LSTM + off + HEAD_DIM]
        a = jnp.exp(col_a[:, hd:hd + 1] + masked_d[hd]) * scores[hd]
        gi = g_inter[:, hd:hd + 1]
        num = _dot(a.astype(BF16), v) + gi * inters[hd][:, 0:HEAD_DIM]
        den = jnp.sum(a, axis=-1, keepdims=True) + gi * inters[hd][:, HEAD_DIM:HEAD_DIM + 1]
        hh = num * (1.0 / jnp.maximum(jnp.abs(den), e_negm[:, hd:hd + 1]))
        hn = _rms(hh) * ghead_ref[:, off:off + HEAD_DIM]
        head_parts.append(hn * _sigmoid(o))
        wk = w_last[:, hd:hd + 1] * ks[hd]
        upd = lax.dot_general(v, wk.astype(BF16), _TN, preferred_element_type=F32)
        gl = g_last[:, hd:hd + 1]
        st_ref[hd, 0:HEAD_DIM, :] = gl * states[hd][0:HEAD_DIM] + upd
        st_ref[hd, HEAD_DIM:HEAD_DIM + 1, :] = (
            gl * states[hd][HEAD_DIM:HEAD_DIM + 1] + jnp.sum(wk, axis=0, keepdims=True))

    mix_in = jnp.concatenate([pool_out] + head_parts, axis=1).astype(BF16)
    x1 = x + gt1 * _dot(mix_in, wout_ref[...])

    h2 = (_rms(x1) * (g2_ref[...] * (1.0 + sc2)) + sh2).astype(BF16)

    def conv_chunk(up, c):
        ue = jnp.concatenate([cbuf_ref[c], up], axis=0)
        wc = wcv_ref[c]
        cbuf_ref[c] = up[valid - CONV_HIST:valid]
        return (wc[3:4] + wc[0:1] * pltpu.roll(ue, 2, axis=0)[CONV_HIST:]
                + wc[1:2] * pltpu.roll(ue, 1, axis=0)[CONV_HIST:] + wc[2:3] * up)

    up_next = _dot(h2, wup_ref[0])
    for j in range(N_FF_CHUNKS):
        up_cur = up_next
        if j + 1 < N_FF_CHUNKS:
            up_next = _dot(h2, wup_ref[j + 1])
        cv = conv_chunk(up_cur, j)
        val, gate = cv[:, 0:FF_CHUNK], cv[:, FF_CHUNK:2 * FF_CHUNK]
        act = (gate * _sigmoid(gate)) * val
        down = _dot(act.astype(BF16), wdown_ref[j])
        if j == 0:
            acc_ref[...] = down
        else:
            acc_ref[...] += down
    x2 = x1 + gt2 * acc_ref[...]
    if final:
        x2 = _rms(x2) * gfin_ref[...]
    y_ref[0] = x2

    @pl.when(t == pl.num_programs(1) - 1)
    def _store_state():
        poolo_ref[0] = hist_ref[...]
        sto_ref[0] = st_ref[...]
        mo_ref[0] = m_ref[...]
        convo_ref[0] = cbuf_ref[...]


def _resident(shape):
    nd = len(shape)
    return pl.BlockSpec(shape, lambda b, t: (0,) * nd, pipeline_mode=pl.Buffered(1))


def _run_layer(x, mod, pool0, st0, m0, conv0, prm, g_final, *, tt, valid, start, final):
    bsz, t_len, _ = x.shape
    nt = t_len // tt
    n_cc = N_FF_CHUNKS
    ff_w = 2 * FF_CHUNK
    per_b3 = lambda b, t: (b, 0, 0)
    per_b4 = lambda b, t: (b, 0, 0, 0)
    in_specs = [
        pl.BlockSpec((1, tt, D_MODEL), lambda b, t: (b, t, 0)),
        pl.BlockSpec((1, 6, D_MODEL), per_b3),
        pl.BlockSpec((1, POOL_HIST, D_POOL), per_b3),
        pl.BlockSpec((1, N_HEADS, STATE_ROWS, HEAD_DIM), per_b4),
        pl.BlockSpec((1, SUBLANES, LANES), per_b3),
        pl.BlockSpec((1, n_cc, CONV_HIST, ff_w), per_b4),
        _resident((1, D_MODEL)),
        _resident((D_MODEL, D_INP)),
        _resident((1, 2 * LANES)),
        _resident((N_POOL_GROUPS, POOL_GROUP, POOL_GROUP)),
        _resident((1, D_POOL)),
        _resident((1, D_LSTM)),
        _resident((D_MODEL, D_MODEL)),
        _resident((1, D_MODEL)),
        _resident((n_cc, D_MODEL, ff_w)),
        _resident((n_cc, SUBLANES, ff_w)),
        _resident((N_FF_CHUNKS, FF_CHUNK, D_MODEL)),
        _resident((1, D_MODEL)),
    ]
    out_specs = [
        pl.BlockSpec((1, tt, D_MODEL), lambda b, t: (b, t, 0)),
        pl.BlockSpec((1, POOL_HIST, D_POOL), per_b3),
        pl.BlockSpec((1, N_HEADS, STATE_ROWS, HEAD_DIM), per_b4),
        pl.BlockSpec((1, SUBLANES, LANES), per_b3),
        pl.BlockSpec((1, n_cc, CONV_HIST, ff_w), per_b4),
    ]
    out_shape = [
        jax.ShapeDtypeStruct((bsz, t_len, D_MODEL), F32),
        jax.ShapeDtypeStruct((bsz, POOL_HIST, D_POOL), F32),
        jax.ShapeDtypeStruct((bsz, N_HEADS, STATE_ROWS, HEAD_DIM), F32),
        jax.ShapeDtypeStruct((bsz, SUBLANES, LANES), F32),
        jax.ShapeDtypeStruct((bsz, n_cc, CONV_HIST, ff_w), F32),
    ]
    scratch = [
        pltpu.VMEM((POOL_HIST, D_POOL), F32),
        pltpu.VMEM((N_HEADS, STATE_ROWS, HEAD_DIM), F32),
        pltpu.VMEM((SUBLANES, LANES), F32),
        pltpu.VMEM((n_cc, CONV_HIST, ff_w), F32),
        pltpu.VMEM((tt, D_INP), F32),
        pltpu.VMEM((tt, D_MODEL), F32),
    ]
    kern = functools.partial(_layer_kernel, tt=tt, valid=valid, start=start, final=final)
    return pl.pallas_call(
        kern,
        grid=(bsz, nt),
        in_specs=in_specs,
        out_specs=out_specs,
        out_shape=out_shape,
        scratch_shapes=scratch,
        compiler_params=pltpu.CompilerParams(
            dimension_semantics=("arbitrary", "arbitrary"),
            vmem_limit_bytes=VMEM_LIMIT_BYTES),
        name="layer_t%d" % tt,
    )(x, mod, pool0, st0, m0, conv0, *prm, g_final)


def _chunk_major(a):
    lead, rows = a.shape[:-2], a.shape[-2]
    nl = len(lead)
    a = a.reshape(lead + (rows, 2, N_FF_CHUNKS, FF_CHUNK))
    a = a.transpose(tuple(range(nl)) + (nl + 2, nl, nl + 1, nl + 3))
    return a.reshape(lead + (N_FF_CHUNKS, rows, 2 * FF_CHUNK))


def _chunk_major_inverse(a):
    lead, rows = a.shape[:-3], a.shape[-2]
    nl = len(lead)
    a = a.reshape(lead + (N_FF_CHUNKS, rows, 2, FF_CHUNK))
    a = a.transpose(tuple(range(nl)) + (nl + 1, nl + 2, nl, nl + 3))
    return a.reshape(lead + (rows, 2 * D_FF))


def _prep_layer_params(g1, w_in, b_gate, w_pool, s_pool, g_head, w_out, g2, w_up, w_conv, b_conv, w_down):
    lane_pad = ((0, 0), (0, LANES - N_HEADS))
    w_in_p = jnp.concatenate(
        [w_in[:, :D_MAIN], jnp.pad(w_in[:, D_MAIN:D_MAIN + N_HEADS], lane_pad),
         jnp.pad(w_in[:, D_MAIN + N_HEADS:], lane_pad)], axis=1).astype(BF16)
    bg = jnp.concatenate([jnp.pad(b_gate[None, :N_HEADS], lane_pad), jnp.pad(b_gate[None, N_HEADS:], lane_pad)], axis=1)
    taps = jnp.concatenate([w_conv, b_conv[None, :], jnp.zeros((SUBLANES - CONV_W - 1, 2 * D_FF), F32)], axis=0)
    w_down_c = w_down.reshape(N_FF_CHUNKS, FF_CHUNK, D_MODEL).astype(BF16)
    return (g1.reshape(1, D_MODEL), w_in_p, bg, w_pool.astype(BF16), s_pool.reshape(1, D_POOL),
            g_head.reshape(1, D_LSTM), w_out.astype(BF16), g2.reshape(1, D_MODEL),
            _chunk_major(w_up.astype(BF16)), _chunk_major(taps), w_down_c)


def _pack_states(pool, c_mat, n_vec, m_vec, conv):
    bsz = pool.shape[0]
    pool_p = jnp.pad(pool, ((0, 0), (POOL_HIST - POOL_BUF, 0), (0, 0)))
    st = jnp.concatenate(
        [c_mat, n_vec[:, :, None, :], jnp.zeros((bsz, N_HEADS, STATE_ROWS - HEAD_DIM - 1, HEAD_DIM), F32)], axis=2)
    m_p = jnp.broadcast_to(
        jnp.pad(m_vec, ((0, 0), (0, LANES - N_HEADS)))[:, None, :], (bsz, SUBLANES, LANES))
    conv_p = _chunk_major(jnp.pad(conv, ((0, 0), (CONV_HIST - CONV_BUF, 0), (0, 0))))
    return pool_p, st, m_p, conv_p


def _unpack_states(pool_p, st, m_p, conv_p):
    conv = _chunk_major_inverse(conv_p)
    return (pool_p[:, POOL_HIST - POOL_BUF:], st[:, :, :HEAD_DIM], st[:, :, HEAD_DIM],
            m_p[:, 0, :N_HEADS], conv[:, CONV_HIST - CONV_BUF:])


def kernel(x_prompt, x_sample, state_pool, state_mlstm_C, state_mlstm_n, state_mlstm_m, state_conv, c_prompt, c_sample, w_ada, b_ada, g_norm1, w_in, b_gate, w_pool, s_pool, g_head, w_out, g_norm2, w_up, w_conv, b_conv, w_down, g_final):
    depth = w_ada.shape[0]
    bp, seq, _ = x_prompt.shape
    bs, dec_seq, _ = x_sample.shape
    tt_p = 256
    tt_s = LANES
    assert seq % tt_p == 0 and POOL_HIST <= dec_seq <= tt_s and dec_seq % SUBLANES == 0

    mod = _adaln_mod(jnp.concatenate([c_prompt, c_sample], axis=0), w_ada, b_ada)
    mod = mod.reshape(depth, bp + bs, 6, D_MODEL)
    gfin = g_final.reshape(1, D_MODEL)

    xp = x_prompt
    xs = jnp.pad(x_sample, ((0, 0), (0, tt_s - dec_seq), (0, 0)))
    zero_states = _pack_states(
        jnp.zeros((bp, POOL_BUF, D_POOL), F32), jnp.zeros((bp, N_HEADS, HEAD_DIM, HEAD_DIM), F32),
        jnp.zeros((bp, N_HEADS, HEAD_DIM), F32), jnp.zeros((bp, N_HEADS), F32),
        jnp.zeros((bp, CONV_BUF, 2 * D_FF), F32))
    outs_p, outs_s = [], []
    for l in range(depth):
        prm = _prep_layer_params(g_norm1[l], w_in[l], b_gate[l], w_pool[l], s_pool[l], g_head[l],
                                 w_out[l], g_norm2[l], w_up[l], w_conv[l], b_conv[l], w_down[l])
        final = l == depth - 1
        res = _run_layer(xp, mod[l, :bp], *zero_states, prm, gfin,
                         tt=tt_p, valid=tt_p, start=0, final=final)
        xp = res[0]
        outs_p.append(_unpack_states(*res[1:]))
        sample_states = _pack_states(state_pool[l], state_mlstm_C[l], state_mlstm_n[l],
                                     state_mlstm_m[l], state_conv[l])
        res = _run_layer(xs, mod[l, bp:], *sample_states, prm, gfin,
                         tt=tt_s, valid=dec_seq, start=PAST_LEN, final=final)
        xs = res[0]
        outs_s.append(_unpack_states(*res[1:]))
    stack = lambda outs, i: jnp.stack([o[i] for o in outs])
    return ((xp, xs[:, :dec_seq])
            + tuple(stack(outs_p, i) for i in range(5))
            + tuple(stack(outs_s, i) for i in range(5)))
```

```python
import functools

import jax
import jax.numpy as jnp
from jax import lax
from jax.experimental import pallas as pl
from jax.experimental.pallas import tpu as pltpu

D_MODEL = 1024
D_POOL = 512
N_POOL_GROUPS = 4
POOL_GROUP = 128
POOL_WINDOWS = (2, 4, 8, 16)
POOL_BUF = 15
D_LSTM = 512
N_HEADS = 4
HEAD_DIM = 128
D_MAIN = D_POOL + 4 * D_LSTM
D_FF = 2816
CONV_W = 3
CONV_BUF = 2
EPS = 1e-6
PAST_LEN = 1024

LANES = 128
SUBLANES = 8
MXU_DIM = 256
VMEM_LIMIT_BYTES = 56 * 1024 * 1024

D_INP = D_MAIN + 2 * LANES
POOL_HIST = 2 * SUBLANES
CONV_HIST = SUBLANES
FF_CHUNK = MXU_DIM
N_FF_CHUNKS = D_FF // FF_CHUNK
MIXER_STAGE_AFTER_CHUNK = {-1: "proj", 0: "gates", 1: "scores", 2: "pool", 3: "weights", 4: "numupd", 5: "heads", 6: "out"}
STATE_ROWS = 2 * HEAD_DIM
NEG_BIG = -1e30

_NT = (((1,), (1,)), ((), ()))
_TN = (((0,), (0,)), ((), ()))
F32 = jnp.float32
BF16 = jnp.bfloat16


def _rms(v):
    return v * lax.rsqrt(jnp.mean(v * v, axis=-1, keepdims=True) + EPS)


def _sigmoid(v):
    return 1.0 / (1.0 + jnp.exp(-v))


def _dot(a, b):
    return jnp.dot(a, b, preferred_element_type=F32)


def _mod_kernel(c_ref, w_ref, b_ref, o_ref):
    c = c_ref[...]
    act = (c * _sigmoid(c)).astype(BF16)
    o_ref[0] = _dot(act, w_ref[0].astype(BF16)) + b_ref[0]


def _adaln_mod(c_all, w_ada, b_ada):
    depth = w_ada.shape[0]
    rows = c_all.shape[0]
    n_out = w_ada.shape[2]
    tn = 1536
    return pl.pallas_call(
        _mod_kernel,
        grid=(depth, n_out // tn),
        in_specs=[
            pl.BlockSpec((rows, D_MODEL), lambda l, j: (0, 0)),
            pl.BlockSpec((1, D_MODEL, tn), lambda l, j: (l, 0, j)),
            pl.BlockSpec((1, 1, tn), lambda l, j: (l, 0, j)),
        ],
        out_specs=pl.BlockSpec((1, rows, tn), lambda l, j: (l, 0, j)),
        out_shape=jax.ShapeDtypeStruct((depth, rows, n_out), F32),
        compiler_params=pltpu.CompilerParams(
            dimension_semantics=("arbitrary", "arbitrary"),
            vmem_limit_bytes=VMEM_LIMIT_BYTES),
        name="adaln_mod",
    )(c_all, w_ada, b_ada.reshape(depth, 1, n_out))


def _layer_kernel(*refs, tt, nt, n_tiles, valid, start, final, carried):
    if carried:
        x_ref, modm_ref, modf_ref, pool0_ref, c0_ref, n0_ref, m0_ref, conv0_ref = refs[:8]
        refs = refs[8:]
    else:
        x_ref, modm_ref, modf_ref = refs[:3]
        refs = refs[3:]
    (g1_ref, win_ref, bg_ref, wpool_ref, spool_ref, ghead_ref, wout_ref,
     g2_ref, wup_ref, wconv_ref, bconv_ref, wdown_ref, gfin_ref,
     y_ref, poolo_ref, co_ref, no_ref, mo_ref, convo_ref,
     hist_ref, st_ref, m_ref, cbuf_ref, acc_ref, ubuf_ref, x1_ref, h2_ref) = refs
    s = pl.program_id(0)
    t_m = lax.rem(jnp.minimum(s, n_tiles - 1), nt)
    t_f = lax.rem(jnp.maximum(s - 1, 0), nt)
    slot_m = lax.rem(s, 2)
    slot_f = 1 - slot_m

    @pl.when(s == 0)
    def _no_previous_tile():
        x1_ref[1] = jnp.zeros((tt, D_MODEL), F32)
        h2_ref[1] = jnp.zeros((tt, D_MODEL), BF16)

    @pl.when(t_m == 0)
    def _load_mixer_state():
        hist_ref[...] = jnp.zeros_like(hist_ref)
        st_ref[...] = jnp.zeros_like(st_ref)
        m_ref[...] = jnp.zeros_like(m_ref)
        if carried:
            hist_ref[POOL_HIST - POOL_BUF:POOL_HIST, :] = pool0_ref[0, 0]
            st_ref[:, 0:HEAD_DIM, :] = c0_ref[0, 0]
            for hd in range(N_HEADS):
                st_ref[hd, HEAD_DIM:HEAD_DIM + 1, :] = n0_ref[0, 0, hd:hd + 1, :]
            m_ref[0:1, 0:N_HEADS] = m0_ref[0, 0]

    @pl.when(t_f == 0)
    def _load_conv_state():
        cbuf_ref[...] = jnp.zeros_like(cbuf_ref)
        if carried:
            cbuf_ref[CONV_HIST - CONV_BUF:CONV_HIST, :] = conv0_ref[0, 0]

    mx = {}

    def m_proj():
        mod = modm_ref[0, 0]
        mx["x"] = x_ref[0]
        hb = (_rms(mx["x"]) * (g1_ref[0] * (1.0 + mod[1:2])) + mod[0:1]).astype(BF16)
        mx["hb"] = hb
        mx["gates"] = _dot(hb, win_ref[0, :, D_MAIN:D_INP])
        mx["qk"] = _dot(hb, win_ref[0, :, D_POOL:D_POOL + 2 * D_LSTM])

    def m_gates():
        rows_tt = lax.broadcasted_iota(jnp.int32, (tt, tt), 0)
        causal = rows_tt >= lax.broadcasted_iota(jnp.int32, (tt, tt), 1)
        ig = mx["gates"][:, 0:LANES] + bg_ref[0, :, 0:LANES]
        fg = mx["gates"][:, LANES:2 * LANES] + bg_ref[0, :, LANES:2 * LANES]
        lf = jnp.minimum(fg, 0.0) - jnp.log1p(jnp.exp(-jnp.abs(fg)))
        lf_hi = lf.astype(BF16)
        rem = lf - lf_hi.astype(F32)
        lf_mid = rem.astype(BF16)
        lf_lo = (rem - lf_mid.astype(F32)).astype(BF16)
        tri = jnp.where(causal, 1.0, 0.0).astype(BF16)
        b2 = _dot(tri, jnp.concatenate([lf_hi, lf_mid], axis=1))
        b = b2[:, 0:LANES] + b2[:, LANES:2 * LANES] + _dot(tri, lf_lo)
        d = ig - b
        mx.update(causal=causal, b=b, d=d, d_rows=d.T)
        hb = mx["hb"]
        mx["vo"] = _dot(hb, win_ref[0, :, D_POOL + 2 * D_LSTM:D_MAIN])
        mx["u"] = _dot(hb, win_ref[0, :, 0:D_POOL])

    def m_scores():
        ks, scores, inters, states = [], [], [], []
        qk = mx["qk"]
        for hd in range(N_HEADS):
            off = hd * HEAD_DIM
            q = qk[:, off:off + HEAD_DIM].astype(BF16)
            k = qk[:, D_LSTM + off:D_LSTM + off + HEAD_DIM] * (HEAD_DIM ** -0.5)
            state = st_ref[hd]
            scores.append(lax.dot_general(q, k.astype(BF16), _NT, preferred_element_type=F32))
            inters.append(lax.dot_general(q, state.astype(BF16), _NT, preferred_element_type=F32))
            ks.append(k)
            states.append(state)
        mx.update(ks=ks, scores=scores, inters=inters, states=states)

    def m_pool():
        u = mx["u"]
        pos = start + t_m * tt + lax.broadcasted_iota(jnp.int32, (tt, LANES), 0)
        ext = jnp.concatenate([hist_ref[...], u], axis=0)
        pool_parts = []
        for g, w in enumerate(POOL_WINDOWS):
            sl = slice(g * POOL_GROUP, (g + 1) * POOL_GROUP)
            c = ext[:, sl]
            shift = 1
            while shift < w:
                c = c + pltpu.roll(c, shift, axis=0)
                shift *= 2
            cnt = jnp.minimum(w, pos + 1).astype(F32)
            diff = c[POOL_HIST:] * (1.0 / cnt) - u[:, sl]
            pool_parts.append(_dot(diff.astype(BF16), wpool_ref[0, g]))
        mx["pool_out"] = jnp.concatenate(pool_parts, axis=1) * spool_ref[0]
        hist_ref[...] = u[valid - POOL_HIST:valid]

    def m_weights():
        b, d, d_rows, causal = mx["b"], mx["d"], mx["d_rows"], mx["causal"]
        lane_id = lax.broadcasted_iota(jnp.int32, (tt, LANES), 1)
        masked_d, gmax = [], jnp.full((tt, LANES), NEG_BIG, F32)
        for hd in range(N_HEADS):
            md = jnp.where(causal, d_rows[hd:hd + 1, :], NEG_BIG)
            masked_d.append(md)
            gmax = jnp.where(lane_id == hd, jnp.max(md, axis=-1, keepdims=True), gmax)
        m_prev = m_ref[0:1, :]
        m_t = b + jnp.maximum(m_prev, gmax)
        col_a = b - m_t
        g_inter = jnp.exp(b + m_prev - m_t)
        last = valid - 1
        row_id = lax.broadcasted_iota(jnp.int32, (tt, LANES), 0)
        w_last = jnp.where(row_id < valid, jnp.exp(col_a[last:last + 1, :] + d), 0.0)
        m_ref[0:1, :] = m_t[last:last + 1, :]
        a_bf, a_sum, wk_bf, wk_sum = [], [], [], []
        for hd in range(N_HEADS):
            a = jnp.exp(col_a[:, hd:hd + 1] + masked_d[hd]) * mx["scores"][hd]
            a_bf.append(a.astype(BF16))
            a_sum.append(jnp.sum(a, axis=-1, keepdims=True))
            wk = w_last[:, hd:hd + 1] * mx["ks"][hd]
            wk_bf.append(wk.astype(BF16))
            wk_sum.append(jnp.sum(wk, axis=0, keepdims=True))
        mx.update(g_inter=g_inter, e_negm=jnp.exp(-m_t), g_last=g_inter[last:last + 1, :],
                  a_bf=a_bf, a_sum=a_sum, wk_bf=wk_bf, wk_sum=wk_sum)

    def m_numupd():
        nums, upds = [], []
        for hd in range(N_HEADS):
            v = mx["vo"][:, hd * HEAD_DIM:(hd + 1) * HEAD_DIM].astype(BF16)
            nums.append(_dot(mx["a_bf"][hd], v))
            upds.append(lax.dot_general(v, mx["wk_bf"][hd], _TN, preferred_element_type=F32))
        mx.update(nums=nums, upds=upds)

    def m_heads():
        head_parts = []
        for hd in range(N_HEADS):
            off = hd * HEAD_DIM
            o = mx["vo"][:, D_LSTM + off:D_LSTM + off + HEAD_DIM]
            inter, state = mx["inters"][hd], mx["states"][hd]
            gi = mx["g_inter"][:, hd:hd + 1]
            num = mx["nums"][hd] + gi * inter[:, 0:HEAD_DIM]
            den = mx["a_sum"][hd] + gi * inter[:, HEAD_DIM:HEAD_DIM + 1]
            hh = num * (1.0 / jnp.maximum(jnp.abs(den), mx["e_negm"][:, hd:hd + 1]))
            hn = _rms(hh) * ghead_ref[0, :, off:off + HEAD_DIM]
            head_parts.append(hn * _sigmoid(o))
            gl = mx["g_last"][:, hd:hd + 1]
            st_ref[hd, 0:HEAD_DIM, :] = gl * state[0:HEAD_DIM] + mx["upds"][hd]
            st_ref[hd, HEAD_DIM:HEAD_DIM + 1, :] = gl * state[HEAD_DIM:HEAD_DIM + 1] + mx["wk_sum"][hd]
        mx["mix_in"] = jnp.concatenate([mx["pool_out"]] + head_parts, axis=1).astype(BF16)

    def m_out():
        mod = modm_ref[0, 0]
        x1 = mx["x"] + mod[2:3] * _dot(mx["mix_in"], wout_ref[0])
        x1_ref[slot_m] = x1
        h2_ref[slot_m] = (_rms(x1) * (g2_ref[0] * (1.0 + mod[4:5])) + mod[3:4]).astype(BF16)

    mixer_stages = dict(proj=m_proj, gates=m_gates, scores=m_scores, pool=m_pool,
                        weights=m_weights, numupd=m_numupd, heads=m_heads, out=m_out)

    h2 = h2_ref[slot_f]

    def up_proj(j):
        c0 = j * FF_CHUNK
        return (_dot(h2, wup_ref[0, :, c0:c0 + FF_CHUNK]),
                _dot(h2, wup_ref[0, :, D_FF + c0:D_FF + c0 + FF_CHUNK]))

    def causal_conv(up, col0, slot, half):
        lanes = slice(half * FF_CHUNK, (half + 1) * FF_CHUNK)
        cols = slice(col0, col0 + FF_CHUNK)
        ubuf_ref[slot, 0:CONV_HIST, lanes] = cbuf_ref[:, cols]
        ubuf_ref[slot, CONV_HIST:CONV_HIST + tt, lanes] = up
        cbuf_ref[:, cols] = up[valid - CONV_HIST:valid]
        w = wconv_ref[0, :, cols]
        return (bconv_ref[0, :, cols]
                + w[0:1] * ubuf_ref[slot, CONV_HIST - 2:CONV_HIST - 2 + tt, lanes]
                + w[1:2] * ubuf_ref[slot, CONV_HIST - 1:CONV_HIST - 1 + tt, lanes]
                + w[2:3] * up)

    ups = {0: up_proj(0)}
    if -1 in MIXER_STAGE_AFTER_CHUNK:
        mixer_stages[MIXER_STAGE_AFTER_CHUNK[-1]]()
    for j in range(N_FF_CHUNKS):
        if j + 1 < N_FF_CHUNKS:
            ups[j + 1] = up_proj(j + 1)
        up_val, up_gate = ups.pop(j)
        val = causal_conv(up_val, j * FF_CHUNK, j % 2, 0)
        gate = causal_conv(up_gate, D_FF + j * FF_CHUNK, j % 2, 1)
        act = ((gate * _sigmoid(gate)) * val).astype(BF16)
        down = _dot(act, wdown_ref[0, j * FF_CHUNK:(j + 1) * FF_CHUNK, :])
        if j == 0:
            acc_ref[...] = down
        else:
            acc_ref[...] += down
        if j in MIXER_STAGE_AFTER_CHUNK:
            mixer_stages[MIXER_STAGE_AFTER_CHUNK[j]]()
    x2 = x1_ref[slot_f] + modf_ref[0, 0][5:6] * acc_ref[...]
    if final:
        x2 = _rms(x2) * gfin_ref[...]
    y_ref[0] = x2

    @pl.when((t_m == nt - 1) & (s < n_tiles))
    def _store_mixer_state():
        poolo_ref[0] = hist_ref[POOL_HIST - POOL_BUF:POOL_HIST, :]
        co_ref[0] = st_ref[:, 0:HEAD_DIM, :]
        for hd in range(N_HEADS):
            no_ref[0, hd:hd + 1, :] = st_ref[hd, HEAD_DIM:HEAD_DIM + 1, :]
        mo_ref[0] = m_ref[0:1, 0:N_HEADS]

    @pl.when((t_f == nt - 1) & (s >= 1))
    def _store_conv_state():
        convo_ref[0] = cbuf_ref[CONV_HIST - CONV_BUF:CONV_HIST, :]


def _layer_weight(shape, layer):
    nd = len(shape)
    return pl.BlockSpec((1,) + tuple(shape), lambda s: (layer,) + (0,) * nd, pipeline_mode=pl.Buffered(1))


def _run_layer(x, mod, mod_row0, states, prm, g_final, layer, *, tt, valid, start, final):
    bsz, t_len, _ = x.shape
    nt = t_len // tt
    n_tiles = bsz * nt
    carried = states is not None
    mix_b = lambda s: jnp.minimum(s, n_tiles - 1) // nt
    mix_t = lambda s: jnp.minimum(s, n_tiles - 1) % nt
    ffn_b = lambda s: jnp.maximum(s - 1, 0) // nt
    ffn_t = lambda s: jnp.maximum(s - 1, 0) % nt
    in_specs = [
        pl.BlockSpec((1, tt, D_MODEL), lambda s: (mix_b(s), mix_t(s), 0)),
        pl.BlockSpec((1, 1, 6, D_MODEL), lambda s: (layer, mod_row0 + mix_b(s), 0, 0)),
        pl.BlockSpec((1, 1, 6, D_MODEL), lambda s: (layer, mod_row0 + ffn_b(s), 0, 0)),
    ]
    if carried:
        in_specs += [
            pl.BlockSpec((1, 1, POOL_BUF, D_POOL), lambda s: (layer, mix_b(s), 0, 0)),
            pl.BlockSpec((1, 1, N_HEADS, HEAD_DIM, HEAD_DIM), lambda s: (layer, mix_b(s), 0, 0, 0)),
            pl.BlockSpec((1, 1, N_HEADS, HEAD_DIM), lambda s: (layer, mix_b(s), 0, 0)),
            pl.BlockSpec((1, 1, 1, N_HEADS), lambda s: (layer, mix_b(s), 0, 0)),
            pl.BlockSpec((1, 1, CONV_BUF, 2 * D_FF), lambda s: (layer, ffn_b(s), 0, 0)),
        ]
    in_specs += [
        _layer_weight((1, D_MODEL), layer),
        _layer_weight((D_MODEL, D_INP), layer),
        _layer_weight((1, 2 * LANES), layer),
        _layer_weight((N_POOL_GROUPS, POOL_GROUP, POOL_GROUP), layer),
        _layer_weight((1, D_POOL), layer),
        _layer_weight((1, D_LSTM), layer),
        _layer_weight((D_MODEL, D_MODEL), layer),
        _layer_weight((1, D_MODEL), layer),
        _layer_weight((D_MODEL, 2 * D_FF), layer),
        _layer_weight((CONV_W, 2 * D_FF), layer),
        _layer_weight((1, 2 * D_FF), layer),
        _layer_weight((D_FF, D_MODEL), layer),
        pl.BlockSpec((1, D_MODEL), lambda s: (0, 0), pipeline_mode=pl.Buffered(1)),
    ]
    out_specs = [
        pl.BlockSpec((1, tt, D_MODEL), lambda s: (ffn_b(s), ffn_t(s), 0)),
        pl.BlockSpec((1, POOL_BUF, D_POOL), lambda s: (mix_b(s), 0, 0)),
        pl.BlockSpec((1, N_HEADS, HEAD_DIM, HEAD_DIM), lambda s: (mix_b(s), 0, 0, 0)),
        pl.BlockSpec((1, N_HEADS, HEAD_DIM), lambda s: (mix_b(s), 0, 0)),
        pl.BlockSpec((1, 1, N_HEADS), lambda s: (mix_b(s), 0, 0)),
        pl.BlockSpec((1, CONV_BUF, 2 * D_FF), lambda s: (ffn_b(s), 0, 0)),
    ]
    out_shape = [
        jax.ShapeDtypeStruct((bsz, t_len, D_MODEL), F32),
        jax.ShapeDtypeStruct((bsz, POOL_BUF, D_POOL), F32),
        jax.ShapeDtypeStruct((bsz, N_HEADS, HEAD_DIM, HEAD_DIM), F32),
        jax.ShapeDtypeStruct((bsz, N_HEADS, HEAD_DIM), F32),
        jax.ShapeDtypeStruct((bsz, 1, N_HEADS), F32),
        jax.ShapeDtypeStruct((bsz, CONV_BUF, 2 * D_FF), F32),
    ]
    scratch = [
        pltpu.VMEM((POOL_HIST, D_POOL), F32),
        pltpu.VMEM((N_HEADS, STATE_ROWS, HEAD_DIM), F32),
        pltpu.VMEM((SUBLANES, LANES), F32),
        pltpu.VMEM((CONV_HIST, 2 * D_FF), F32),
        pltpu.VMEM((tt, D_MODEL), F32),
        pltpu.VMEM((2, CONV_HIST + tt, 2 * FF_CHUNK), F32),
        pltpu.VMEM((2, tt, D_MODEL), F32),
        pltpu.VMEM((2, tt, D_MODEL), BF16),
    ]
    kern = functools.partial(_layer_kernel, tt=tt, nt=nt, n_tiles=n_tiles, valid=valid, start=start,
                             final=final, carried=carried)
    args = (x, mod, mod) + (tuple(states) if carried else ()) + tuple(prm) + (g_final,)
    return pl.pallas_call(
        kern,
        grid=(n_tiles + 1,),
        in_specs=in_specs,
        out_specs=out_specs,
        out_shape=out_shape,
        scratch_shapes=scratch,
        compiler_params=pltpu.CompilerParams(
            dimension_semantics=("arbitrary",),
            vmem_limit_bytes=VMEM_LIMIT_BYTES),
        name="layer_t%d" % tt,
    )(*args)


def _prep_params(g_norm1, w_in, b_gate, w_pool, s_pool, g_head, w_out, g_norm2, w_up, w_conv, b_conv, w_down):
    depth = w_in.shape[0]
    lane_pad = ((0, 0), (0, 0), (0, LANES - N_HEADS))
    w_in_p = jnp.concatenate(
        [w_in[:, :, :D_MAIN], jnp.pad(w_in[:, :, D_MAIN:D_MAIN + N_HEADS], lane_pad),
         jnp.pad(w_in[:, :, D_MAIN + N_HEADS:], lane_pad)], axis=2).astype(BF16)
    bg3 = b_gate[:, None, :]
    bg = jnp.concatenate([jnp.pad(bg3[:, :, :N_HEADS], lane_pad), jnp.pad(bg3[:, :, N_HEADS:], lane_pad)], axis=2)
    return (g_norm1.reshape(depth, 1, D_MODEL), w_in_p, bg, w_pool.astype(BF16), s_pool.reshape(depth, 1, D_POOL),
            g_head.reshape(depth, 1, D_LSTM), w_out.astype(BF16), g_norm2.reshape(depth, 1, D_MODEL),
            w_up.astype(BF16), w_conv, b_conv.reshape(depth, 1, 2 * D_FF), w_down.astype(BF16))


def kernel(x_prompt, x_sample, state_pool, state_mlstm_C, state_mlstm_n, state_mlstm_m, state_conv, c_prompt, c_sample, w_ada, b_ada, g_norm1, w_in, b_gate, w_pool, s_pool, g_head, w_out, g_norm2, w_up, w_conv, b_conv, w_down, g_final):
    depth = w_ada.shape[0]
    bp, seq, _ = x_prompt.shape
    bs, dec_seq, _ = x_sample.shape
    tt_p = 256
    tt_s = LANES
    assert seq % tt_p == 0 and POOL_HIST <= dec_seq <= tt_s and dec_seq % SUBLANES == 0

    mod = _adaln_mod(jnp.concatenate([c_prompt, c_sample], axis=0), w_ada, b_ada)
    mod = mod.reshape(depth, bp + bs, 6, D_MODEL)
    gfin = g_final.reshape(1, D_MODEL)
    prm = _prep_params(g_norm1, w_in, b_gate, w_pool, s_pool, g_head, w_out, g_norm2, w_up, w_conv, b_conv, w_down)
    sample_states = (state_pool, state_mlstm_C, state_mlstm_n,
                     state_mlstm_m.reshape(depth, bs, 1, N_HEADS), state_conv)

    xp = x_prompt
    xs = jnp.pad(x_sample, ((0, 0), (0, tt_s - dec_seq), (0, 0)))
    outs_p, outs_s = [], []
    for l in range(depth):
        final = l == depth - 1
        res = _run_layer(xp, mod, 0, None, prm, gfin, l, tt=tt_p, valid=tt_p, start=0, final=final)
        xp = res[0]
        outs_p.append(res[1:])
        res = _run_layer(xs, mod, bp, sample_states, prm, gfin, l,
                         tt=tt_s, valid=dec_seq, start=PAST_LEN, final=final)
        xs = res[0]
        outs_s.append(res[1:])

    def stacked(outs):
        pool, c_mat, n_vec, m_vec, conv = (jnp.stack([o[i] for o in outs]) for i in range(5))
        return pool, c_mat, n_vec, m_vec.reshape(depth, -1, N_HEADS), conv

    return (xp, xs[:, :dec_seq]) + stacked(outs_p) + stacked(outs_s)
```

```python
import functools

import jax
import jax.numpy as jnp
from jax import lax
from jax.experimental import pallas as pl
from jax.experimental.pallas import tpu as pltpu

D_MODEL = 1024
D_POOL = 512
N_POOL_GROUPS = 4
POOL_GROUP = 128
POOL_WINDOWS = (2, 4, 8, 16)
POOL_BUF = 15
D_LSTM = 512
N_HEADS = 4
HEAD_DIM = 128
D_MAIN = D_POOL + 4 * D_LSTM
D_FF = 2816
CONV_W = 3
CONV_BUF = 2
EPS = 1e-6
PAST_LEN = 1024

LANES = 128
SUBLANES = 8
MXU_DIM = 256
VMEM_LIMIT_BYTES = 56 * 1024 * 1024

D_INP = D_MAIN + 2 * LANES
POOL_HIST = 2 * SUBLANES
CONV_HIST = SUBLANES
FF_CHUNK = MXU_DIM
N_FF_CHUNKS = D_FF // FF_CHUNK
MIXER_STAGES_AFTER_CHUNK = {-1: ("proj",), 0: ("gates",), 1: ("scores",), 2: ("pool",), 3: ("weights",),
                            4: ("numupd",), 5: ("heads",), 6: ("out",)}
STATE_ROWS = 2 * HEAD_DIM
NEG_BIG = -1e30

_NT = (((1,), (1,)), ((), ()))
_TN = (((0,), (0,)), ((), ()))
F32 = jnp.float32
BF16 = jnp.bfloat16


def _rms(v):
    return v * lax.rsqrt(jnp.mean(v * v, axis=-1, keepdims=True) + EPS)


def _sigmoid(v):
    return 1.0 / (1.0 + jnp.exp(-v))


def _dot(a, b):
    return jnp.dot(a, b, preferred_element_type=F32)


def _mod_kernel(c_ref, w_ref, b_ref, o_ref):
    c = c_ref[...]
    act = (c * _sigmoid(c)).astype(BF16)
    o_ref[0] = _dot(act, w_ref[0].astype(BF16)) + b_ref[0]


def _adaln_mod(c_all, w_ada, b_ada):
    depth = w_ada.shape[0]
    rows = c_all.shape[0]
    n_out = w_ada.shape[2]
    tn = 1536
    return pl.pallas_call(
        _mod_kernel,
        grid=(depth, n_out // tn),
        in_specs=[
            pl.BlockSpec((rows, D_MODEL), lambda l, j: (0, 0)),
            pl.BlockSpec((1, D_MODEL, tn), lambda l, j: (l, 0, j)),
            pl.BlockSpec((1, 1, tn), lambda l, j: (l, 0, j)),
        ],
        out_specs=pl.BlockSpec((1, rows, tn), lambda l, j: (l, 0, j)),
        out_shape=jax.ShapeDtypeStruct((depth, rows, n_out), F32),
        compiler_params=pltpu.CompilerParams(
            dimension_semantics=("arbitrary", "arbitrary"),
            vmem_limit_bytes=VMEM_LIMIT_BYTES),
        name="adaln_mod",
    )(c_all, w_ada, b_ada.reshape(depth, 1, n_out))


def _layer_kernel(*refs, seg, sl, nt, n_tiles, start, final, carried):
    tt = seg * sl
    if carried:
        x_ref, modm_ref, modf_ref, pool0_ref, c0_ref, n0_ref, m0_ref, conv0_ref = refs[:8]
        refs = refs[8:]
    else:
        x_ref, modm_ref, modf_ref = refs[:3]
        refs = refs[3:]
    (g1_ref, win_ref, bg_ref, wpool_ref, spool_ref, ghead_ref, wout_ref,
     g2_ref, wup_ref, wconv_ref, bconv_ref, wdown_ref, gfin_ref,
     y_ref, poolo_ref, co_ref, no_ref, mo_ref, convo_ref,
     hist_ref, st_ref, m_ref, cbuf_ref, acc_ref, ubuf_ref, x1_ref, h2_ref) = refs
    s = pl.program_id(0)
    t_m = lax.rem(jnp.minimum(s, n_tiles - 1), nt)
    t_f = lax.rem(jnp.maximum(s - 1, 0), nt)
    slot_m = lax.rem(s, 2)
    slot_f = 1 - slot_m
    seg_rows = [slice(g * sl, (g + 1) * sl) for g in range(seg)]

    def per_row(vals):
        if seg == 1:
            return vals[0]
        return jnp.concatenate([jnp.broadcast_to(v, (sl, v.shape[-1])) for v in vals], axis=0)

    def mod_rows(mod_ref, r):
        return [mod_ref[0, g, r:r + 1, :] for g in range(seg)]

    @pl.when(s == 0)
    def _no_previous_tile():
        x1_ref[1] = jnp.zeros((tt, D_MODEL), F32)
        h2_ref[1] = jnp.zeros((tt, D_MODEL), BF16)

    @pl.when(t_m == 0)
    def _load_mixer_state():
        hist_ref[...] = jnp.zeros_like(hist_ref)
        st_ref[...] = jnp.zeros_like(st_ref)
        m_ref[...] = jnp.zeros_like(m_ref)
        if carried:
            for g in range(seg):
                hist_ref[g, POOL_HIST - POOL_BUF:POOL_HIST, :] = pool0_ref[0, g]
                st_ref[g, :, 0:HEAD_DIM, :] = c0_ref[0, g]
                for hd in range(N_HEADS):
                    st_ref[g, hd, HEAD_DIM:HEAD_DIM + 1, :] = n0_ref[0, g, hd:hd + 1, :]
                m_ref[g:g + 1, 0:N_HEADS] = m0_ref[0, g]

    @pl.when(t_f == 0)
    def _load_conv_state():
        cbuf_ref[...] = jnp.zeros_like(cbuf_ref)
        if carried:
            for g in range(seg):
                cbuf_ref[g, CONV_HIST - CONV_BUF:CONV_HIST, :] = conv0_ref[0, g]

    mx = {}

    def m_proj():
        mx["x"] = x_ref[0]
        scale = per_row([g1_ref[0] * (1.0 + sc) for sc in mod_rows(modm_ref, 1)])
        hb = (_rms(mx["x"]) * scale + per_row(mod_rows(modm_ref, 0))).astype(BF16)
        mx["hb"] = hb
        mx["gates"] = _dot(hb, win_ref[0, :, D_MAIN:D_INP])
        mx["qk"] = _dot(hb, win_ref[0, :, D_POOL:D_POOL + 2 * D_LSTM])

    def m_gates():
        row = lax.broadcasted_iota(jnp.int32, (tt, tt), 0)
        col = lax.broadcasted_iota(jnp.int32, (tt, tt), 1)
        causal = row >= col
        for g in range(1, seg):
            causal = causal & ((row < g * sl) | (col >= g * sl))
        ig = mx["gates"][:, 0:LANES] + bg_ref[0, :, 0:LANES]
        fg = mx["gates"][:, LANES:2 * LANES] + bg_ref[0, :, LANES:2 * LANES]
        lf = jnp.minimum(fg, 0.0) - jnp.log1p(jnp.exp(-jnp.abs(fg)))
        lf_hi = lf.astype(BF16)
        rem = lf - lf_hi.astype(F32)
        lf_mid = rem.astype(BF16)
        lf_lo = (rem - lf_mid.astype(F32)).astype(BF16)
        tri = jnp.where(causal, 1.0, 0.0).astype(BF16)
        b2 = _dot(tri, jnp.concatenate([lf_hi, lf_mid], axis=1))
        b = b2[:, 0:LANES] + b2[:, LANES:2 * LANES] + _dot(tri, lf_lo)
        d = ig - b
        mx.update(causal=causal, b=b, d=d, d_rows=d.T)
        hb = mx["hb"]
        mx["vo"] = _dot(hb, win_ref[0, :, D_POOL + 2 * D_LSTM:D_MAIN])
        mx["u"] = _dot(hb, win_ref[0, :, 0:D_POOL])

    def m_scores():
        ks, scores, inters = [], [], []
        qk = mx["qk"]
        for hd in range(N_HEADS):
            off = hd * HEAD_DIM
            q = qk[:, off:off + HEAD_DIM].astype(BF16)
            k = qk[:, D_LSTM + off:D_LSTM + off + HEAD_DIM] * (HEAD_DIM ** -0.5)
            scores.append(lax.dot_general(q, k.astype(BF16), _NT, preferred_element_type=F32))
            parts = [lax.dot_general(q[seg_rows[g]], st_ref[g, hd].astype(BF16), _NT, preferred_element_type=F32)
                     for g in range(seg)]
            inters.append(parts[0] if seg == 1 else jnp.concatenate(parts, axis=0))
            ks.append(k)
        mx.update(ks=ks, scores=scores, inters=inters)

    def m_pool():
        u = mx["u"]
        pos = start + t_m * sl + lax.broadcasted_iota(jnp.int32, (sl, LANES), 0)
        diffs = [[] for _ in POOL_WINDOWS]
        for g in range(seg):
            u_g = u[seg_rows[g]]
            ext = jnp.concatenate([hist_ref[g], u_g], axis=0)
            for i, w in enumerate(POOL_WINDOWS):
                sl_c = slice(i * POOL_GROUP, (i + 1) * POOL_GROUP)
                c = ext[:, sl_c]
                shift = 1
                while shift < w:
                    c = c + pltpu.roll(c, shift, axis=0)
                    shift *= 2
                cnt = jnp.minimum(w, pos + 1).astype(F32)
                diffs[i].append(c[POOL_HIST:] * (1.0 / cnt) - u_g[:, sl_c])
            hist_ref[g] = u_g[sl - POOL_HIST:sl]
        pool_parts = []
        for i in range(len(POOL_WINDOWS)):
            diff = diffs[i][0] if seg == 1 else jnp.concatenate(diffs[i], axis=0)
            pool_parts.append(_dot(diff.astype(BF16), wpool_ref[0, i]))
        mx["pool_out"] = jnp.concatenate(pool_parts, axis=1) * spool_ref[0]

    def m_weights():
        b, d, d_rows, causal = mx["b"], mx["d"], mx["d_rows"], mx["causal"]
        lane_id = lax.broadcasted_iota(jnp.int32, (tt, LANES), 1)
        masked_d, gmax = [], jnp.full((tt, LANES), NEG_BIG, F32)
        for hd in range(N_HEADS):
            md = jnp.where(causal, d_rows[hd:hd + 1, :], NEG_BIG)
            masked_d.append(md)
            gmax = jnp.where(lane_id == hd, jnp.max(md, axis=-1, keepdims=True), gmax)
        m_prev = per_row([m_ref[g:g + 1, :] for g in range(seg)])
        m_t = b + jnp.maximum(m_prev, gmax)
        col_a = b - m_t
        g_inter = jnp.exp(b + m_prev - m_t)
        lasts = [(g + 1) * sl - 1 for g in range(seg)]
        w_last = jnp.exp(per_row([col_a[r:r + 1, :] for r in lasts]) + d)
        for g in range(seg):
            m_ref[g:g + 1, :] = m_t[lasts[g]:lasts[g] + 1, :]
        a_bf, a_sum, wk_bf, wk_sum = [], [], [], []
        for hd in range(N_HEADS):
            a = jnp.exp(col_a[:, hd:hd + 1] + masked_d[hd]) * mx["scores"][hd]
            a_bf.append(a.astype(BF16))
            a_sum.append(jnp.sum(a, axis=-1, keepdims=True))
            wk = w_last[:, hd:hd + 1] * mx["ks"][hd]
            wk_bf.append(wk.astype(BF16))
            wk_sum.append([jnp.sum(wk[seg_rows[g]], axis=0, keepdims=True) for g in range(seg)])
        mx.update(g_inter=g_inter, e_negm=jnp.exp(-m_t), g_last=[g_inter[r:r + 1, :] for r in lasts],
                  a_bf=a_bf, a_sum=a_sum, wk_bf=wk_bf, wk_sum=wk_sum)

    def m_numupd():
        nums, upds = [], []
        for hd in range(N_HEADS):
            v = mx["vo"][:, hd * HEAD_DIM:(hd + 1) * HEAD_DIM].astype(BF16)
            nums.append(_dot(mx["a_bf"][hd], v))
            upds.append([lax.dot_general(v[seg_rows[g]], mx["wk_bf"][hd][seg_rows[g]], _TN,
                                         preferred_element_type=F32) for g in range(seg)])
        mx.update(nums=nums, upds=upds)

    def m_heads():
        head_parts = []
        for hd in range(N_HEADS):
            off = hd * HEAD_DIM
            o = mx["vo"][:, D_LSTM + off:D_LSTM + off + HEAD_DIM]
            inter = mx["inters"][hd]
            gi = mx["g_inter"][:, hd:hd + 1]
            num = mx["nums"][hd] + gi * inter[:, 0:HEAD_DIM]
            den = mx["a_sum"][hd] + gi * inter[:, HEAD_DIM:HEAD_DIM + 1]
            hh = num * (1.0 / jnp.maximum(jnp.abs(den), mx["e_negm"][:, hd:hd + 1]))
            hn = _rms(hh) * ghead_ref[0, :, off:off + HEAD_DIM]
            head_parts.append(hn * _sigmoid(o))
            for g in range(seg):
                gl = mx["g_last"][g][:, hd:hd + 1]
                state = st_ref[g, hd]
                st_ref[g, hd, 0:HEAD_DIM, :] = gl * state[0:HEAD_DIM] + mx["upds"][hd][g]
                st_ref[g, hd, HEAD_DIM:HEAD_DIM + 1, :] = gl * state[HEAD_DIM:HEAD_DIM + 1] + mx["wk_sum"][hd][g]
        mx["mix_in"] = jnp.concatenate([mx["pool_out"]] + head_parts, axis=1).astype(BF16)

    def m_out():
        x1 = mx["x"] + per_row(mod_rows(modm_ref, 2)) * _dot(mx["mix_in"], wout_ref[0])
        x1_ref[slot_m] = x1
        scale = per_row([g2_ref[0] * (1.0 + sc) for sc in mod_rows(modm_ref, 4)])
        h2_ref[slot_m] = (_rms(x1) * scale + per_row(mod_rows(modm_ref, 3))).astype(BF16)

    mixer_stages = dict(proj=m_proj, gates=m_gates, scores=m_scores, pool=m_pool,
                        weights=m_weights, numupd=m_numupd, heads=m_heads, out=m_out)

    h2 = h2_ref[slot_f]
    stage_rows = CONV_HIST + sl

    def up_proj(j):
        c0 = j * FF_CHUNK
        return (_dot(h2, wup_ref[0, :, c0:c0 + FF_CHUNK]),
                _dot(h2, wup_ref[0, :, D_FF + c0:D_FF + c0 + FF_CHUNK]))

    def causal_conv(up, col0, slot, half):
        lanes = slice(half * FF_CHUNK, (half + 1) * FF_CHUNK)
        cols = slice(col0, col0 + FF_CHUNK)
        prev2, prev1 = [], []
        for g in range(seg):
            base = g * stage_rows
            up_g = up[seg_rows[g]]
            ubuf_ref[slot, base:base + CONV_HIST, lanes] = cbuf_ref[g, :, cols]
            ubuf_ref[slot, base + CONV_HIST:base + stage_rows, lanes] = up_g
            cbuf_ref[g, :, cols] = up_g[sl - CONV_HIST:sl]
            prev2.append(ubuf_ref[slot, base + CONV_HIST - 2:base + CONV_HIST - 2 + sl, lanes])
            prev1.append(ubuf_ref[slot, base + CONV_HIST - 1:base + CONV_HIST - 1 + sl, lanes])
        if seg > 1:
            prev2, prev1 = [jnp.concatenate(prev2, axis=0)], [jnp.concatenate(prev1, axis=0)]
        w = wconv_ref[0, :, cols]
        return bconv_ref[0, :, cols] + w[0:1] * prev2[0] + w[1:2] * prev1[0] + w[2:3] * up

    ups = {0: up_proj(0)}
    for name in MIXER_STAGES_AFTER_CHUNK.get(-1, ()):
        mixer_stages[name]()
    for j in range(N_FF_CHUNKS):
        if j + 1 < N_FF_CHUNKS:
            ups[j + 1] = up_proj(j + 1)
        up_val, up_gate = ups.pop(j)
        val = causal_conv(up_val, j * FF_CHUNK, j % 2, 0)
        gate = causal_conv(up_gate, D_FF + j * FF_CHUNK, j % 2, 1)
        act = ((gate * _sigmoid(gate)) * val).astype(BF16)
        down = _dot(act, wdown_ref[0, j * FF_CHUNK:(j + 1) * FF_CHUNK, :])
        if j == 0:
            acc_ref[...] = down
        else:
            acc_ref[...] += down
        for name in MIXER_STAGES_AFTER_CHUNK.get(j, ()):
            mixer_stages[name]()
    x2 = x1_ref[slot_f] + per_row(mod_rows(modf_ref, 5)) * acc_ref[...]
    if final:
        x2 = _rms(x2) * gfin_ref[...]
    y_ref[0] = x2

    @pl.when((t_m == nt - 1) & (s < n_tiles))
    def _store_mixer_state():
        for g in range(seg):
            poolo_ref[g] = hist_ref[g, POOL_HIST - POOL_BUF:POOL_HIST, :]
            co_ref[g] = st_ref[g, :, 0:HEAD_DIM, :]
            for hd in range(N_HEADS):
                no_ref[g, hd:hd + 1, :] = st_ref[g, hd, HEAD_DIM:HEAD_DIM + 1, :]
            mo_ref[g] = m_ref[g:g + 1, 0:N_HEADS]

    @pl.when((t_f == nt - 1) & (s >= 1))
    def _store_conv_state():
        for g in range(seg):
            convo_ref[g] = cbuf_ref[g, CONV_HIST - CONV_BUF:CONV_HIST, :]


def _layer_weight(shape, layer):
    nd = len(shape)
    return pl.BlockSpec((1,) + tuple(shape), lambda s: (layer,) + (0,) * nd, pipeline_mode=pl.Buffered(1))


def _run_layer(x, mod, mod_row0, states, prm, g_final, layer, *, seg, sl, start, final):
    n_seq, t_len, _ = x.shape
    assert n_seq % seg == 0 and t_len % sl == 0 and mod_row0 % seg == 0 and (seg == 1 or t_len == sl)
    assert sl >= POOL_HIST and sl % SUBLANES == 0 and seg <= SUBLANES
    tt = seg * sl
    nt = t_len // sl
    n_groups = n_seq // seg
    n_tiles = n_groups * nt
    carried = states is not None
    xg = x.reshape(n_groups, seg * t_len, D_MODEL)
    mix_b = lambda s: jnp.minimum(s, n_tiles - 1) // nt
    mix_t = lambda s: jnp.minimum(s, n_tiles - 1) % nt
    ffn_b = lambda s: jnp.maximum(s - 1, 0) // nt
    ffn_t = lambda s: jnp.maximum(s - 1, 0) % nt
    mod_g0 = mod_row0 // seg
    in_specs = [
        pl.BlockSpec((1, tt, D_MODEL), lambda s: (mix_b(s), mix_t(s), 0)),
        pl.BlockSpec((1, seg, 6, D_MODEL), lambda s: (layer, mod_g0 + mix_b(s), 0, 0)),
        pl.BlockSpec((1, seg, 6, D_MODEL), lambda s: (layer, mod_g0 + ffn_b(s), 0, 0)),
    ]
    if carried:
        in_specs += [
            pl.BlockSpec((1, seg, POOL_BUF, D_POOL), lambda s: (layer, mix_b(s), 0, 0)),
            pl.BlockSpec((1, seg, N_HEADS, HEAD_DIM, HEAD_DIM), lambda s: (layer, mix_b(s), 0, 0, 0)),
            pl.BlockSpec((1, seg, N_HEADS, HEAD_DIM), lambda s: (layer, mix_b(s), 0, 0)),
            pl.BlockSpec((1, seg, 1, N_HEADS), lambda s: (layer, mix_b(s), 0, 0)),
            pl.BlockSpec((1, seg, CONV_BUF, 2 * D_FF), lambda s: (layer, ffn_b(s), 0, 0)),
        ]
    in_specs += [
        _layer_weight((1, D_MODEL), layer),
        _layer_weight((D_MODEL, D_INP), layer),
        _layer_weight((1, 2 * LANES), layer),
        _layer_weight((N_POOL_GROUPS, POOL_GROUP, POOL_GROUP), layer),
        _layer_weight((1, D_POOL), layer),
        _layer_weight((1, D_LSTM), layer),
        _layer_weight((D_MODEL, D_MODEL), layer),
        _layer_weight((1, D_MODEL), layer),
        _layer_weight((D_MODEL, 2 * D_FF), layer),
        _layer_weight((CONV_W, 2 * D_FF), layer),
        _layer_weight((1, 2 * D_FF), layer),
        _layer_weight((D_FF, D_MODEL), layer),
        pl.BlockSpec((1, D_MODEL), lambda s: (0, 0), pipeline_mode=pl.Buffered(1)),
    ]
    out_specs = [
        pl.BlockSpec((1, tt, D_MODEL), lambda s: (ffn_b(s), ffn_t(s), 0)),
        pl.BlockSpec((seg, POOL_BUF, D_POOL), lambda s: (mix_b(s), 0, 0)),
        pl.BlockSpec((seg, N_HEADS, HEAD_DIM, HEAD_DIM), lambda s: (mix_b(s), 0, 0, 0)),
        pl.BlockSpec((seg, N_HEADS, HEAD_DIM), lambda s: (mix_b(s), 0, 0)),
        pl.BlockSpec((seg, 1, N_HEADS), lambda s: (mix_b(s), 0, 0)),
        pl.BlockSpec((seg, CONV_BUF, 2 * D_FF), lambda s: (ffn_b(s), 0, 0)),
    ]
    out_shape = [
        jax.ShapeDtypeStruct((n_groups, seg * t_len, D_MODEL), F32),
        jax.ShapeDtypeStruct((n_seq, POOL_BUF, D_POOL), F32),
        jax.ShapeDtypeStruct((n_seq, N_HEADS, HEAD_DIM, HEAD_DIM), F32),
        jax.ShapeDtypeStruct((n_seq, N_HEADS, HEAD_DIM), F32),
        jax.ShapeDtypeStruct((n_seq, 1, N_HEADS), F32),
        jax.ShapeDtypeStruct((n_seq, CONV_BUF, 2 * D_FF), F32),
    ]
    scratch = [
        pltpu.VMEM((seg, POOL_HIST, D_POOL), F32),
        pltpu.VMEM((seg, N_HEADS, STATE_ROWS, HEAD_DIM), F32),
        pltpu.VMEM((SUBLANES, LANES), F32),
        pltpu.VMEM((seg, CONV_HIST, 2 * D_FF), F32),
        pltpu.VMEM((tt, D_MODEL), F32),
        pltpu.VMEM((2, seg * (CONV_HIST + sl), 2 * FF_CHUNK), F32),
        pltpu.VMEM((2, tt, D_MODEL), F32),
        pltpu.VMEM((2, tt, D_MODEL), BF16),
    ]
    kern = functools.partial(_layer_kernel, seg=seg, sl=sl, nt=nt, n_tiles=n_tiles, start=start,
                             final=final, carried=carried)
    args = (xg, mod, mod) + (tuple(states) if carried else ()) + tuple(prm) + (g_final,)
    res = pl.pallas_call(
        kern,
        grid=(n_tiles + 1,),
        in_specs=in_specs,
        out_specs=out_specs,
        out_shape=out_shape,
        scratch_shapes=scratch,
        compiler_params=pltpu.CompilerParams(
            dimension_semantics=("arbitrary",),
            vmem_limit_bytes=VMEM_LIMIT_BYTES),
        name="layer_t%d" % tt,
    )(*args)
    return (res[0].reshape(n_seq, t_len, D_MODEL),) + tuple(res[1:])


def _prep_params(g_norm1, w_in, b_gate, w_pool, s_pool, g_head, w_out, g_norm2, w_up, w_conv, b_conv, w_down):
    depth = w_in.shape[0]
    lane_pad = ((0, 0), (0, 0), (0, LANES - N_HEADS))
    w_in_p = jnp.concatenate(
        [w_in[:, :, :D_MAIN], jnp.pad(w_in[:, :, D_MAIN:D_MAIN + N_HEADS], lane_pad),
         jnp.pad(w_in[:, :, D_MAIN + N_HEADS:], lane_pad)], axis=2).astype(BF16)
    bg3 = b_gate[:, None, :]
    bg = jnp.concatenate([jnp.pad(bg3[:, :, :N_HEADS], lane_pad), jnp.pad(bg3[:, :, N_HEADS:], lane_pad)], axis=2)
    return (g_norm1.reshape(depth, 1, D_MODEL), w_in_p, bg, w_pool.astype(BF16), s_pool.reshape(depth, 1, D_POOL),
            g_head.reshape(depth, 1, D_LSTM), w_out.astype(BF16), g_norm2.reshape(depth, 1, D_MODEL),
            w_up.astype(BF16), w_conv, b_conv.reshape(depth, 1, 2 * D_FF), w_down.astype(BF16))


def kernel(x_prompt, x_sample, state_pool, state_mlstm_C, state_mlstm_n, state_mlstm_m, state_conv, c_prompt, c_sample, w_ada, b_ada, g_norm1, w_in, b_gate, w_pool, s_pool, g_head, w_out, g_norm2, w_up, w_conv, b_conv, w_down, g_final):
    depth = w_ada.shape[0]
    bp, seq, _ = x_prompt.shape
    bs, dec_seq, _ = x_sample.shape
    tile_rows = MXU_DIM
    seg_s = max(1, min(tile_rows // dec_seq, SUBLANES))
    assert seq % tile_rows == 0 and bs % seg_s == 0 and bp % seg_s == 0

    mod = _adaln_mod(jnp.concatenate([c_prompt, c_sample], axis=0), w_ada, b_ada)
    mod = mod.reshape(depth, bp + bs, 6, D_MODEL)
    gfin = g_final.reshape(1, D_MODEL)
    prm = _prep_params(g_norm1, w_in, b_gate, w_pool, s_pool, g_head, w_out, g_norm2, w_up, w_conv, b_conv, w_down)
    sample_states = (state_pool, state_mlstm_C, state_mlstm_n,
                     state_mlstm_m.reshape(depth, bs, 1, N_HEADS), state_conv)

    xp, xs = x_prompt, x_sample
    outs_p, outs_s = [], []
    for l in range(depth):
        final = l == depth - 1
        res = _run_layer(xp, mod, 0, None, prm, gfin, l, seg=1, sl=tile_rows, start=0, final=final)
        xp = res[0]
        outs_p.append(res[1:])
        res = _run_layer(xs, mod, bp, sample_states, prm, gfin, l, seg=seg_s, sl=dec_seq, start=PAST_LEN,
                         final=final)
        xs = res[0]
        outs_s.append(res[1:])

    def stacked(outs):
        pool, c_mat, n_vec, m_vec, conv = (jnp.stack([o[i] for o in outs]) for i in range(5))
        return pool, c_mat, n_vec, m_vec.reshape(depth, -1, N_HEADS), conv

    return (xp, xs) + stacked(outs_p) + stacked(outs_s)
```

```python
import functools

import jax
import jax.numpy as jnp
from jax import lax
from jax.experimental import pallas as pl
from jax.experimental.pallas import tpu as pltpu

D_MODEL = 1024
D_POOL = 512
N_POOL_GROUPS = 4
POOL_GROUP = 128
POOL_WINDOWS = (2, 4, 8, 16)
POOL_BUF = 15
D_LSTM = 512
N_HEADS = 4
HEAD_DIM = 128
D_MAIN = D_POOL + 4 * D_LSTM
D_FF = 2816
CONV_W = 3
CONV_BUF = 2
EPS = 1e-6
PAST_LEN = 1024

LANES = 128
SUBLANES = 8
MXU_DIM = 256
VMEM_LIMIT_BYTES = 56 * 1024 * 1024

D_INP = D_MAIN + LANES
POOL_HIST = 2 * SUBLANES
CONV_HIST = SUBLANES
FF_CHUNK = MXU_DIM
N_FF_CHUNKS = D_FF // FF_CHUNK
MIXER_STAGES_AFTER_CHUNK = {-1: ("proj",), 0: ("gates",), 1: ("scores",), 2: ("pool",), 3: ("weights",),
                            4: ("numupd",), 5: ("heads",), 6: ("out",)}
PROMPT_TILES_PER_STEP = 2
TILE_OVERLAP_CHUNKS = 2
FF_LOOKAHEAD = 1
STATE_ROWS = 2 * HEAD_DIM
NEG_BIG = -1e30

_NT = (((1,), (1,)), ((), ()))
_TN = (((0,), (0,)), ((), ()))
F32 = jnp.float32
BF16 = jnp.bfloat16


def _rms(v):
    return v * lax.rsqrt(jnp.mean(v * v, axis=-1, keepdims=True) + EPS)


def _sigmoid(v):
    return 1.0 / (1.0 + jnp.exp(-v))


def _dot(a, b):
    return jnp.dot(a, b, preferred_element_type=F32)


def _mod_kernel(c_ref, w_ref, b_ref, o_ref):
    c = c_ref[...]
    act = (c * _sigmoid(c)).astype(BF16)
    o_ref[0] = _dot(act, w_ref[0].astype(BF16)) + b_ref[0]


def _adaln_mod(c_all, w_ada, b_ada):
    depth = w_ada.shape[0]
    rows = c_all.shape[0]
    n_out = w_ada.shape[2]
    tn = 1536
    return pl.pallas_call(
        _mod_kernel,
        grid=(depth, n_out // tn),
        in_specs=[
            pl.BlockSpec((rows, D_MODEL), lambda l, j: (0, 0)),
            pl.BlockSpec((1, D_MODEL, tn), lambda l, j: (l, 0, j)),
            pl.BlockSpec((1, 1, tn), lambda l, j: (l, 0, j)),
        ],
        out_specs=pl.BlockSpec((1, rows, tn), lambda l, j: (l, 0, j)),
        out_shape=jax.ShapeDtypeStruct((depth, rows, n_out), F32),
        compiler_params=pltpu.CompilerParams(
            dimension_semantics=("arbitrary", "arbitrary"),
            vmem_limit_bytes=VMEM_LIMIT_BYTES),
        name="adaln_mod",
    )(c_all, w_ada, b_ada.reshape(depth, 1, n_out))


def _layer_kernel(*refs, seg, sl, nt, unroll, n_blocks, start, final, carried):
    tt = seg * sl
    if carried:
        x_ref, modm_ref, modf_ref, pool0_ref, c0_ref, n0_ref, m0_ref, conv0_ref = refs[:8]
        refs = refs[8:]
    else:
        x_ref, modm_ref, modf_ref = refs[:3]
        refs = refs[3:]
    (g1_ref, win_ref, bg_ref, wpool_ref, spool_ref, ghead_ref, wout_ref,
     g2_ref, wup_ref, wconv_ref, bconv_ref, wdown_ref, gfin_ref,
     y_ref, poolo_ref, co_ref, no_ref, mo_ref, convo_ref,
     hist_ref, st_ref, m_ref, cbuf_ref, acc_ref, ubuf_ref, x1_ref, h2_ref) = refs
    s = pl.program_id(0)
    first_m = jnp.minimum(s, n_blocks - 1) * unroll
    first_f = jnp.maximum(s - 1, 0) * unroll
    slot_m = lax.rem(s, 2)
    slot_f = 1 - slot_m
    seg_rows = [slice(g * sl, (g + 1) * sl) for g in range(seg)]

    def per_row(vals):
        if seg == 1:
            return vals[0]
        return jnp.concatenate([jnp.broadcast_to(v, (sl, v.shape[-1])) for v in vals], axis=0)

    def mod_rows(mod_ref, r):
        return [mod_ref[0, g, r:r + 1, :] for g in range(seg)]

    @pl.when(lax.rem(first_m, nt) == 0)
    def _load_mixer_state():
        hist_ref[...] = jnp.zeros_like(hist_ref)
        st_ref[...] = jnp.zeros_like(st_ref)
        m_ref[...] = jnp.zeros_like(m_ref)
        if carried:
            for g in range(seg):
                hist_ref[g, POOL_HIST - POOL_BUF:POOL_HIST, :] = pool0_ref[0, g]
                st_ref[g, :, 0:HEAD_DIM, :] = c0_ref[0, g]
                for hd in range(N_HEADS):
                    st_ref[g, hd, HEAD_DIM:HEAD_DIM + 1, :] = n0_ref[0, g, hd:hd + 1, :]
                m_ref[g:g + 1, 0:N_HEADS] = m0_ref[0, g]

    @pl.when(lax.rem(first_f, nt) == 0)
    def _load_conv_state():
        cbuf_ref[...] = jnp.zeros_like(cbuf_ref)
        if carried:
            for g in range(seg):
                cbuf_ref[g, CONV_HIST - CONV_BUF:CONV_HIST, :] = conv0_ref[0, g]

    def tile_step(r, do_mixer, do_ffn):
        rows = slice(r * tt, (r + 1) * tt)
        t_m = lax.rem(first_m + r, nt)
        mx = {}

        def m_proj():
            mx["x"] = x_ref[0, rows]
            scale = per_row([g1_ref[0] * (1.0 + sc) for sc in mod_rows(modm_ref, 1)])
            hb = (_rms(mx["x"]) * scale + per_row(mod_rows(modm_ref, 0))).astype(BF16)
            mx["hb"] = hb
            mx["gates"] = _dot(hb, win_ref[0, :, D_MAIN:D_INP])
            mx["qk"] = _dot(hb, win_ref[0, :, D_POOL:D_POOL + 2 * D_LSTM])

        def m_gates():
            row = lax.broadcasted_iota(jnp.int32, (tt, tt), 0)
            col = lax.broadcasted_iota(jnp.int32, (tt, tt), 1)
            causal = row >= col
            for g in range(1, seg):
                causal = causal & ((row < g * sl) | (col >= g * sl))
            ig = mx["gates"] + bg_ref[0]
            fg = pltpu.roll(ig, LANES - N_HEADS, axis=1)
            lf = jnp.minimum(fg, 0.0) - jnp.log1p(jnp.exp(-jnp.abs(fg)))
            lf_hi = lf.astype(BF16)
            rem = lf - lf_hi.astype(F32)
            lf_mid = rem.astype(BF16)
            lf_lo = (rem - lf_mid.astype(F32)).astype(BF16)
            tri = jnp.where(causal, 1.0, 0.0).astype(BF16)
            b2 = _dot(tri, jnp.concatenate([lf_hi, lf_mid], axis=1))
            b = b2[:, 0:LANES] + b2[:, LANES:2 * LANES] + _dot(tri, lf_lo)
            d = ig - b
            mx.update(causal=causal, b=b, d=d, d_rows=d.T)
            hb = mx["hb"]
            mx["vo"] = _dot(hb, win_ref[0, :, D_POOL + 2 * D_LSTM:D_MAIN])
            mx["u"] = _dot(hb, win_ref[0, :, 0:D_POOL])

        def m_scores():
            ks, scores, inters = [], [], []
            qk = mx["qk"]
            for hd in range(N_HEADS):
                off = hd * HEAD_DIM
                q = qk[:, off:off + HEAD_DIM].astype(BF16)
                k = qk[:, D_LSTM + off:D_LSTM + off + HEAD_DIM] * (HEAD_DIM ** -0.5)
                scores.append(lax.dot_general(q, k.astype(BF16), _NT, preferred_element_type=F32))
                parts = [lax.dot_general(q[seg_rows[g]], st_ref[g, hd].astype(BF16), _NT, preferred_element_type=F32)
                         for g in range(seg)]
                inters.append(parts[0] if seg == 1 else jnp.concatenate(parts, axis=0))
                ks.append(k)
            mx.update(ks=ks, scores=scores, inters=inters)

        def m_pool():
            u = mx["u"]
            pos = start + t_m * sl + lax.broadcasted_iota(jnp.int32, (sl, LANES), 0)
            diffs = [[] for _ in POOL_WINDOWS]
            for g in range(seg):
                u_g = u[seg_rows[g]]
                ext = jnp.concatenate([hist_ref[g], u_g], axis=0)
                for i, w in enumerate(POOL_WINDOWS):
                    sl_c = slice(i * POOL_GROUP, (i + 1) * POOL_GROUP)
                    c = ext[:, sl_c]
                    shift = 1
                    while shift < w:
                        c = c + pltpu.roll(c, shift, axis=0)
                        shift *= 2
                    cnt = jnp.minimum(w, pos + 1).astype(F32)
                    diffs[i].append(c[POOL_HIST:] * (1.0 / cnt) - u_g[:, sl_c])
                hist_ref[g] = u_g[sl - POOL_HIST:sl]
            pool_parts = []
            for i in range(len(POOL_WINDOWS)):
                diff = diffs[i][0] if seg == 1 else jnp.concatenate(diffs[i], axis=0)
                pool_parts.append(_dot(diff.astype(BF16), wpool_ref[0, i]))
            mx["pool_out"] = jnp.concatenate(pool_parts, axis=1) * spool_ref[0]

        def m_weights():
            b, d, d_rows, causal = mx["b"], mx["d"], mx["d_rows"], mx["causal"]
            lane_id = lax.broadcasted_iota(jnp.int32, (tt, LANES), 1)
            masked_d, gmax = [], jnp.full((tt, LANES), NEG_BIG, F32)
            for hd in range(N_HEADS):
                md = jnp.where(causal, d_rows[hd:hd + 1, :], NEG_BIG)
                masked_d.append(md)
                gmax = jnp.where(lane_id == hd, jnp.max(md, axis=-1, keepdims=True), gmax)
            m_prev = per_row([m_ref[g:g + 1, :] for g in range(seg)])
            m_t = b + jnp.maximum(m_prev, gmax)
            col_a = b - m_t
            g_inter = jnp.exp(b + m_prev - m_t)
            lasts = [(g + 1) * sl - 1 for g in range(seg)]
            w_last = jnp.exp(per_row([col_a[i:i + 1, :] for i in lasts]) + d)
            for g in range(seg):
                m_ref[g:g + 1, :] = m_t[lasts[g]:lasts[g] + 1, :]
            a_bf, a_sum, wk_bf, wk_sum = [], [], [], []
            for hd in range(N_HEADS):
                a = jnp.exp(col_a[:, hd:hd + 1] + masked_d[hd]) * mx["scores"][hd]
                a_bf.append(a.astype(BF16))
                a_sum.append(jnp.sum(a, axis=-1, keepdims=True))
                wk = w_last[:, hd:hd + 1] * mx["ks"][hd]
                wk_bf.append(wk.astype(BF16))
                wk_sum.append([jnp.sum(wk[seg_rows[g]], axis=0, keepdims=True) for g in range(seg)])
            mx.update(g_inter=g_inter, e_negm=jnp.exp(-m_t), g_last=[g_inter[i:i + 1, :] for i in lasts],
                      a_bf=a_bf, a_sum=a_sum, wk_bf=wk_bf, wk_sum=wk_sum)

        def m_numupd():
            nums, upds = [], []
            for hd in range(N_HEADS):
                v = mx["vo"][:, hd * HEAD_DIM:(hd + 1) * HEAD_DIM].astype(BF16)
                nums.append(_dot(mx["a_bf"][hd], v))
                upds.append([lax.dot_general(v[seg_rows[g]], mx["wk_bf"][hd][seg_rows[g]], _TN,
                                             preferred_element_type=F32) for g in range(seg)])
            mx.update(nums=nums, upds=upds)

        def m_heads():
            head_parts = []
            for hd in range(N_HEADS):
                off = hd * HEAD_DIM
                o = mx["vo"][:, D_LSTM + off:D_LSTM + off + HEAD_DIM]
                inter = mx["inters"][hd]
                gi = mx["g_inter"][:, hd:hd + 1]
                num = mx["nums"][hd] + gi * inter[:, 0:HEAD_DIM]
                den = mx["a_sum"][hd] + gi * inter[:, HEAD_DIM:HEAD_DIM + 1]
                hh = num * (1.0 / jnp.maximum(jnp.abs(den), mx["e_negm"][:, hd:hd + 1]))
                hn = _rms(hh) * ghead_ref[0, :, off:off + HEAD_DIM]
                head_parts.append(hn * _sigmoid(o))
                for g in range(seg):
                    gl = mx["g_last"][g][:, hd:hd + 1]
                    state = st_ref[g, hd]
                    st_ref[g, hd, 0:HEAD_DIM, :] = gl * state[0:HEAD_DIM] + mx["upds"][hd][g]
                    st_ref[g, hd, HEAD_DIM:HEAD_DIM + 1, :] = gl * state[HEAD_DIM:HEAD_DIM + 1] + mx["wk_sum"][hd][g]
            mx["mix_in"] = jnp.concatenate([mx["pool_out"]] + head_parts, axis=1).astype(BF16)

        def m_out():
            x1 = mx["x"] + per_row(mod_rows(modm_ref, 2)) * _dot(mx["mix_in"], wout_ref[0])
            x1_ref[slot_m, rows] = x1
            scale = per_row([g2_ref[0] * (1.0 + sc) for sc in mod_rows(modm_ref, 4)])
            h2_ref[slot_m, rows] = (_rms(x1) * scale + per_row(mod_rows(modm_ref, 3))).astype(BF16)

        mixer_stages = dict(proj=m_proj, gates=m_gates, scores=m_scores, pool=m_pool,
                            weights=m_weights, numupd=m_numupd, heads=m_heads, out=m_out)

        h2 = h2_ref[slot_f, rows] if do_ffn else None
        stage_rows = 2 * CONV_HIST + sl

        def up_proj(j):
            c0 = j * FF_CHUNK
            return (_dot(h2, wup_ref[0, :, c0:c0 + FF_CHUNK]),
                    _dot(h2, wup_ref[0, :, D_FF + c0:D_FF + c0 + FF_CHUNK]))

        def causal_conv(up, col0, slot, half):
            lanes = slice(half * FF_CHUNK, (half + 1) * FF_CHUNK)
            cols = slice(col0, col0 + FF_CHUNK)
            prev2, prev1 = [], []
            for g in range(seg):
                base = g * stage_rows
                up_g = up[seg_rows[g]]
                hist = cbuf_ref[g, :, cols]
                for k, prev in ((1, prev1), (2, prev2)):
                    ubuf_ref[slot, k - 1, base + k:base + k + CONV_HIST, lanes] = hist
                    ubuf_ref[slot, k - 1, base + k + CONV_HIST:base + k + CONV_HIST + sl, lanes] = up_g
                    prev.append(ubuf_ref[slot, k - 1, base + CONV_HIST:base + CONV_HIST + sl, lanes])
                cbuf_ref[g, :, cols] = up_g[sl - CONV_HIST:sl]
            if seg > 1:
                prev2, prev1 = [jnp.concatenate(prev2, axis=0)], [jnp.concatenate(prev1, axis=0)]
            w = wconv_ref[0, :, cols]
            return bconv_ref[0, :, cols] + w[0:1] * prev2[0] + w[1:2] * prev1[0] + w[2:3] * up

        def run_stages(after_chunk):
            if do_mixer:
                for name in MIXER_STAGES_AFTER_CHUNK.get(after_chunk, ()):
                    mixer_stages[name]()

        ups = {j: up_proj(j) for j in range(FF_LOOKAHEAD)} if do_ffn else {}
        run_stages(-1)
        yield
        for j in range(N_FF_CHUNKS):
            if do_ffn:
                if j + FF_LOOKAHEAD < N_FF_CHUNKS:
                    ups[j + FF_LOOKAHEAD] = up_proj(j + FF_LOOKAHEAD)
                up_val, up_gate = ups.pop(j)
                val = causal_conv(up_val, j * FF_CHUNK, 2 * r + j % 2, 0)
                gate = causal_conv(up_gate, D_FF + j * FF_CHUNK, 2 * r + j % 2, 1)
                act = ((gate * _sigmoid(gate)) * val).astype(BF16)
                down = _dot(act, wdown_ref[0, j * FF_CHUNK:(j + 1) * FF_CHUNK, :])
                if j == 0:
                    acc_ref[r] = down
                else:
                    acc_ref[r] += down
            run_stages(j)
            if j + 1 < N_FF_CHUNKS:
                yield
        if do_ffn:
            x2 = x1_ref[slot_f, rows] + per_row(mod_rows(modf_ref, 5)) * acc_ref[r]
            if final:
                x2 = _rms(x2) * gfin_ref[...]
            y_ref[0, rows] = x2

    def run_block(do_mixer, do_ffn):
        tiles = [tile_step(r, do_mixer, do_ffn) for r in range(unroll)]
        n_yields = N_FF_CHUNKS
        for r, tile in enumerate(tiles):
            done = TILE_OVERLAP_CHUNKS if r > 0 else 0
            for _ in range(n_yields - done):
                next(tile)
            if r + 1 < unroll:
                for _ in range(TILE_OVERLAP_CHUNKS):
                    next(tiles[r + 1])
            for _ in tile:
                pass

    lax.cond(s == 0, lambda: run_block(True, False),
             lambda: lax.cond(s == n_blocks, lambda: run_block(False, True), lambda: run_block(True, True)))

    @pl.when((lax.rem(first_m + unroll - 1, nt) == nt - 1) & (s < n_blocks))
    def _store_mixer_state():
        for g in range(seg):
            poolo_ref[g] = hist_ref[g, POOL_HIST - POOL_BUF:POOL_HIST, :]
            co_ref[g] = st_ref[g, :, 0:HEAD_DIM, :]
            for hd in range(N_HEADS):
                no_ref[g, hd:hd + 1, :] = st_ref[g, hd, HEAD_DIM:HEAD_DIM + 1, :]
            mo_ref[g] = m_ref[g:g + 1, 0:N_HEADS]

    @pl.when((lax.rem(first_f + unroll - 1, nt) == nt - 1) & (s >= 1))
    def _store_conv_state():
        for g in range(seg):
            convo_ref[g] = cbuf_ref[g, CONV_HIST - CONV_BUF:CONV_HIST, :]


def _layer_weight(shape, layer):
    nd = len(shape)
    return pl.BlockSpec((1,) + tuple(shape), lambda s: (layer,) + (0,) * nd, pipeline_mode=pl.Buffered(1))


def _run_layer(x, mod, mod_row0, states, prm, g_final, layer, *, seg, sl, unroll, start, final):
    n_seq, t_len, _ = x.shape
    assert n_seq % seg == 0 and t_len % sl == 0 and mod_row0 % seg == 0 and (seg == 1 or t_len == sl)
    assert sl >= POOL_HIST and sl % SUBLANES == 0 and seg <= SUBLANES
    tt = seg * sl
    nt = t_len // sl
    n_groups = n_seq // seg
    assert nt % unroll == 0
    nb = nt // unroll
    n_blocks = n_groups * nb
    carried = states is not None
    xg = x.reshape(n_groups, seg * t_len, D_MODEL)
    mix_b = lambda s: jnp.minimum(s, n_blocks - 1) // nb
    mix_t = lambda s: jnp.minimum(s, n_blocks - 1) % nb
    ffn_b = lambda s: jnp.maximum(s - 1, 0) // nb
    ffn_t = lambda s: jnp.maximum(s - 1, 0) % nb
    mod_g0 = mod_row0 // seg
    in_specs = [
        pl.BlockSpec((1, unroll * tt, D_MODEL), lambda s: (mix_b(s), mix_t(s), 0)),
        pl.BlockSpec((1, seg, 6, D_MODEL), lambda s: (layer, mod_g0 + mix_b(s), 0, 0)),
        pl.BlockSpec((1, seg, 6, D_MODEL), lambda s: (layer, mod_g0 + ffn_b(s), 0, 0)),
    ]
    if carried:
        in_specs += [
            pl.BlockSpec((1, seg, POOL_BUF, D_POOL), lambda s: (layer, mix_b(s), 0, 0)),
            pl.BlockSpec((1, seg, N_HEADS, HEAD_DIM, HEAD_DIM), lambda s: (layer, mix_b(s), 0, 0, 0)),
            pl.BlockSpec((1, seg, N_HEADS, HEAD_DIM), lambda s: (layer, mix_b(s), 0, 0)),
            pl.BlockSpec((1, seg, 1, N_HEADS), lambda s: (layer, mix_b(s), 0, 0)),
            pl.BlockSpec((1, seg, CONV_BUF, 2 * D_FF), lambda s: (layer, ffn_b(s), 0, 0)),
        ]
    in_specs += [
        _layer_weight((1, D_MODEL), layer),
        _layer_weight((D_MODEL, D_INP), layer),
        _layer_weight((1, LANES), layer),
        _layer_weight((N_POOL_GROUPS, POOL_GROUP, POOL_GROUP), layer),
        _layer_weight((1, D_POOL), layer),
        _layer_weight((1, D_LSTM), layer),
        _layer_weight((D_MODEL, D_MODEL), layer),
        _layer_weight((1, D_MODEL), layer),
        _layer_weight((D_MODEL, 2 * D_FF), layer),
        _layer_weight((CONV_W, 2 * D_FF), layer),
        _layer_weight((1, 2 * D_FF), layer),
        _layer_weight((D_FF, D_MODEL), layer),
        pl.BlockSpec((1, D_MODEL), lambda s: (0, 0), pipeline_mode=pl.Buffered(1)),
    ]
    out_specs = [
        pl.BlockSpec((1, unroll * tt, D_MODEL), lambda s: (ffn_b(s), ffn_t(s), 0)),
        pl.BlockSpec((seg, POOL_BUF, D_POOL), lambda s: (mix_b(s), 0, 0)),
        pl.BlockSpec((seg, N_HEADS, HEAD_DIM, HEAD_DIM), lambda s: (mix_b(s), 0, 0, 0)),
        pl.BlockSpec((seg, N_HEADS, HEAD_DIM), lambda s: (mix_b(s), 0, 0)),
        pl.BlockSpec((seg, 1, N_HEADS), lambda s: (mix_b(s), 0, 0)),
        pl.BlockSpec((seg, CONV_BUF, 2 * D_FF), lambda s: (ffn_b(s), 0, 0)),
    ]
    out_shape = [
        jax.ShapeDtypeStruct((n_groups, seg * t_len, D_MODEL), F32),
        jax.ShapeDtypeStruct((n_seq, POOL_BUF, D_POOL), F32),
        jax.ShapeDtypeStruct((n_seq, N_HEADS, HEAD_DIM, HEAD_DIM), F32),
        jax.ShapeDtypeStruct((n_seq, N_HEADS, HEAD_DIM), F32),
        jax.ShapeDtypeStruct((n_seq, 1, N_HEADS), F32),
        jax.ShapeDtypeStruct((n_seq, CONV_BUF, 2 * D_FF), F32),
    ]
    scratch = [
        pltpu.VMEM((seg, POOL_HIST, D_POOL), F32),
        pltpu.VMEM((seg, N_HEADS, STATE_ROWS, HEAD_DIM), F32),
        pltpu.VMEM((SUBLANES, LANES), F32),
        pltpu.VMEM((seg, CONV_HIST, 2 * D_FF), F32),
        pltpu.VMEM((unroll, tt, D_MODEL), F32),
        pltpu.VMEM((2 * unroll, CONV_W - 1, seg * (2 * CONV_HIST + sl), 2 * FF_CHUNK), F32),
        pltpu.VMEM((2, unroll * tt, D_MODEL), F32),
        pltpu.VMEM((2, unroll * tt, D_MODEL), BF16),
    ]
    kern = functools.partial(_layer_kernel, seg=seg, sl=sl, nt=nt, unroll=unroll, n_blocks=n_blocks, start=start,
                             final=final, carried=carried)
    args = (xg, mod, mod) + (tuple(states) if carried else ()) + tuple(prm) + (g_final,)
    res = pl.pallas_call(
        kern,
        grid=(n_blocks + 1,),
        in_specs=in_specs,
        out_specs=out_specs,
        out_shape=out_shape,
        scratch_shapes=scratch,
        compiler_params=pltpu.CompilerParams(
            dimension_semantics=("arbitrary",),
            vmem_limit_bytes=VMEM_LIMIT_BYTES),
        name="layer_t%d" % tt,
    )(*args)
    return (res[0].reshape(n_seq, t_len, D_MODEL),) + tuple(res[1:])


def _prep_params(g_norm1, w_in, b_gate, w_pool, s_pool, g_head, w_out, g_norm2, w_up, w_conv, b_conv, w_down):
    depth = w_in.shape[0]
    lane_pad = ((0, 0), (0, 0), (0, LANES - 2 * N_HEADS))
    w_in_p = jnp.pad(w_in, lane_pad).astype(BF16)
    bg = jnp.pad(b_gate[:, None, :], lane_pad)
    return (g_norm1.reshape(depth, 1, D_MODEL), w_in_p, bg, w_pool.astype(BF16), s_pool.reshape(depth, 1, D_POOL),
            g_head.reshape(depth, 1, D_LSTM), w_out.astype(BF16), g_norm2.reshape(depth, 1, D_MODEL),
            w_up.astype(BF16), w_conv, b_conv.reshape(depth, 1, 2 * D_FF), w_down.astype(BF16))


def kernel(x_prompt, x_sample, state_pool, state_mlstm_C, state_mlstm_n, state_mlstm_m, state_conv, c_prompt, c_sample, w_ada, b_ada, g_norm1, w_in, b_gate, w_pool, s_pool, g_head, w_out, g_norm2, w_up, w_conv, b_conv, w_down, g_final):
    depth = w_ada.shape[0]
    bp, seq, _ = x_prompt.shape
    bs, dec_seq, _ = x_sample.shape
    tile_rows = MXU_DIM
    seg_s = max(1, min(tile_rows // dec_seq, SUBLANES))
    assert seq % tile_rows == 0 and bs % seg_s == 0 and bp % seg_s == 0

    mod = _adaln_mod(jnp.concatenate([c_prompt, c_sample], axis=0), w_ada, b_ada)
    mod = mod.reshape(depth, bp + bs, 6, D_MODEL)
    gfin = g_final.reshape(1, D_MODEL)
    prm = _prep_params(g_norm1, w_in, b_gate, w_pool, s_pool, g_head, w_out, g_norm2, w_up, w_conv, b_conv, w_down)
    sample_states = (state_pool, state_mlstm_C, state_mlstm_n,
                     state_mlstm_m.reshape(depth, bs, 1, N_HEADS), state_conv)

    xp, xs = x_prompt, x_sample
    outs_p, outs_s = [], []
    for l in range(depth):
        final = l == depth - 1
        res = _run_layer(xp, mod, 0, None, prm, gfin, l, seg=1, sl=tile_rows, unroll=PROMPT_TILES_PER_STEP, start=0,
                         final=final)
        xp = res[0]
        outs_p.append(res[1:])
        res = _run_layer(xs, mod, bp, sample_states, prm, gfin, l, seg=seg_s, sl=dec_seq, unroll=1, start=PAST_LEN,
                         final=final)
        xs = res[0]
        outs_s.append(res[1:])

    def stacked(outs):
        pool, c_mat, n_vec, m_vec, conv = (jnp.stack([o[i] for o in outs]) for i in range(5))
        return pool, c_mat, n_vec, m_vec.reshape(depth, -1, N_HEADS), conv

    return (xp, xs) + stacked(outs_p) + stacked(outs_s)
```

```python
import functools

import jax
import jax.numpy as jnp
from jax import lax
from jax.experimental import pallas as pl
from jax.experimental.pallas import tpu as pltpu

D_MODEL = 1024
D_POOL = 512
N_POOL_GROUPS = 4
POOL_GROUP = 128
POOL_WINDOWS = (2, 4, 8, 16)
POOL_BUF = 15
D_LSTM = 512
N_HEADS = 4
HEAD_DIM = 128
D_MAIN = D_POOL + 4 * D_LSTM
D_FF = 2816
CONV_W = 3
CONV_BUF = 2
EPS = 1e-6
PAST_LEN = 1024

LANES = 128
SUBLANES = 8
MXU_DIM = 256
VMEM_LIMIT_BYTES = 56 * 1024 * 1024

GATE_ROWS = 2 * SUBLANES
POOL_HIST = 2 * SUBLANES
CONV_HIST = SUBLANES
FF_CHUNK = MXU_DIM
N_FF_CHUNKS = D_FF // FF_CHUNK
MIXER_STAGES_AFTER_CHUNK = {-1: ("proj",), 0: ("gates",), 1: ("scores",), 2: ("pool",), 3: ("weights",),
                            4: ("numupd",), 5: ("heads",), 6: ("out",)}
PROMPT_TILES_PER_STEP = 2
TILE_OVERLAP_CHUNKS = 2
FF_LOOKAHEAD = 1
STATE_ROWS = 2 * HEAD_DIM
NEG_BIG = -1e30

_NT = (((1,), (1,)), ((), ()))
_TN = (((0,), (0,)), ((), ()))
F32 = jnp.float32
BF16 = jnp.bfloat16


def _rms(v):
    return v * lax.rsqrt(jnp.mean(v * v, axis=-1, keepdims=True) + EPS)


def _sigmoid(v):
    return 1.0 / (1.0 + jnp.exp(-v))


def _dot(a, b):
    return jnp.dot(a, b, preferred_element_type=F32)


def _mod_kernel(c_ref, w_ref, b_ref, o_ref):
    c = c_ref[...]
    act = (c * _sigmoid(c)).astype(BF16)
    o_ref[0] = _dot(act, w_ref[0].astype(BF16)) + b_ref[0]


def _adaln_mod(c_all, w_ada, b_ada):
    depth = w_ada.shape[0]
    rows = c_all.shape[0]
    n_out = w_ada.shape[2]
    tn = 1536
    return pl.pallas_call(
        _mod_kernel,
        grid=(depth, n_out // tn),
        in_specs=[
            pl.BlockSpec((rows, D_MODEL), lambda l, j: (0, 0)),
            pl.BlockSpec((1, D_MODEL, tn), lambda l, j: (l, 0, j)),
            pl.BlockSpec((1, 1, tn), lambda l, j: (l, 0, j)),
        ],
        out_specs=pl.BlockSpec((1, rows, tn), lambda l, j: (l, 0, j)),
        out_shape=jax.ShapeDtypeStruct((depth, rows, n_out), F32),
        compiler_params=pltpu.CompilerParams(
            dimension_semantics=("arbitrary", "arbitrary"),
            vmem_limit_bytes=VMEM_LIMIT_BYTES),
        name="adaln_mod",
    )(c_all, w_ada, b_ada.reshape(depth, 1, n_out))


def _layer_kernel(*refs, seg, sl, nt, unroll, n_blocks, start, final, carried):
    tt = seg * sl
    if carried:
        x_ref, modm_ref, modf_ref, pool0_ref, c0_ref, n0_ref, m0_ref, conv0_ref = refs[:8]
        refs = refs[8:]
    else:
        x_ref, modm_ref, modf_ref = refs[:3]
        refs = refs[3:]
    (g1_ref, win_ref, wgt_ref, bg_ref, wpool_ref, spool_ref, ghead_ref, wout_ref,
     g2_ref, wup_ref, wconv_ref, bconv_ref, wdown_ref, gfin_ref,
     y_ref, poolo_ref, co_ref, no_ref, mo_ref, convo_ref,
     hist_ref, st_ref, m_ref, cbuf_ref, acc_ref, ubuf_ref, x1_ref, h2_ref) = refs
    s = pl.program_id(0)
    first_m = jnp.minimum(s, n_blocks - 1) * unroll
    first_f = jnp.maximum(s - 1, 0) * unroll
    slot_m = lax.rem(s, 2)
    slot_f = 1 - slot_m
    seg_rows = [slice(g * sl, (g + 1) * sl) for g in range(seg)]

    def per_row(vals):
        if seg == 1:
            return vals[0]
        return jnp.concatenate([jnp.broadcast_to(v, (sl, v.shape[-1])) for v in vals], axis=0)

    def mod_rows(mod_ref, r):
        return [mod_ref[0, g, r:r + 1, :] for g in range(seg)]

    @pl.when(s == 0)
    def _no_previous_tile():
        x1_ref[1] = jnp.zeros((unroll * tt, D_MODEL), F32)
        h2_ref[1] = jnp.zeros((unroll * tt, D_MODEL), BF16)

    @pl.when(lax.rem(first_m, nt) == 0)
    def _load_mixer_state():
        hist_ref[...] = jnp.zeros_like(hist_ref)
        st_ref[...] = jnp.zeros_like(st_ref)
        m_ref[...] = jnp.zeros_like(m_ref)
        if carried:
            for g in range(seg):
                hist_ref[g, POOL_HIST - POOL_BUF:POOL_HIST, :] = pool0_ref[0, g]
                st_ref[g, :, 0:HEAD_DIM, :] = c0_ref[0, g]
                for hd in range(N_HEADS):
                    st_ref[g, hd, HEAD_DIM:HEAD_DIM + 1, :] = n0_ref[0, g, hd:hd + 1, :]
                m_ref[g:g + 1, 0:N_HEADS] = m0_ref[0, g]

    @pl.when(lax.rem(first_f, nt) == 0)
    def _load_conv_state():
        cbuf_ref[...] = jnp.zeros_like(cbuf_ref)
        if carried:
            for g in range(seg):
                cbuf_ref[g, CONV_HIST - CONV_BUF:CONV_HIST, :] = conv0_ref[0, g]

    def tile_step(r):
        rows = slice(r * tt, (r + 1) * tt)
        t_m = lax.rem(first_m + r, nt)
        mx = {}

        def m_proj():
            mx["x"] = x_ref[0, rows]
            scale = per_row([g1_ref[0] * (1.0 + sc) for sc in mod_rows(modm_ref, 1)])
            hb = (_rms(mx["x"]) * scale + per_row(mod_rows(modm_ref, 0))).astype(BF16)
            mx["hb"] = hb
            mx["gates_t"] = lax.dot_general(wgt_ref[0], hb, _NT, preferred_element_type=F32)
            mx["qk"] = _dot(hb, win_ref[0, :, D_POOL:D_POOL + 2 * D_LSTM])

        def m_gates():
            row = lax.broadcasted_iota(jnp.int32, (tt, tt), 0)
            col = lax.broadcasted_iota(jnp.int32, (tt, tt), 1)
            causal = row >= col
            for g in range(1, seg):
                causal = causal & ((row < g * sl) | (col >= g * sl))
            before = row <= col
            for g in range(1, seg):
                before = before & ((col < g * sl) | (row >= g * sl))
            gates = mx["gates_t"] + jnp.concatenate([bg_ref[0]] * (tt // LANES), axis=1)
            ig = gates[0:SUBLANES]
            fg = gates[SUBLANES:GATE_ROWS]
            lf = jnp.minimum(fg, 0.0) - jnp.log1p(jnp.exp(-jnp.abs(fg)))
            lf_hi = lf.astype(BF16).astype(F32)
            rem = lf - lf_hi
            lf_mid = rem.astype(BF16).astype(F32)
            lf_lo = rem - lf_mid
            pieces = jnp.concatenate([lf_hi, lf_mid, lf_lo, jnp.zeros_like(lf)], axis=0).astype(BF16)
            sums = _dot(pieces, jnp.where(before, 1.0, 0.0).astype(BF16))
            b_t = sums[0:SUBLANES] + sums[SUBLANES:2 * SUBLANES] + sums[2 * SUBLANES:3 * SUBLANES]
            d_rows = ig - b_t
            pad_rows = jnp.zeros((LANES - SUBLANES, tt), F32)
            b = jnp.concatenate([b_t, pad_rows], axis=0).T
            d = jnp.concatenate([d_rows, pad_rows], axis=0).T
            mx.update(causal=causal, b=b, d=d, d_rows=d_rows)
            hb = mx["hb"]
            mx["vo"] = _dot(hb, win_ref[0, :, D_POOL + 2 * D_LSTM:D_MAIN])
            mx["u"] = _dot(hb, win_ref[0, :, 0:D_POOL])

        def m_scores():
            ks, scores, inters = [], [], []
            qk = mx["qk"]
            for hd in range(N_HEADS):
                off = hd * HEAD_DIM
                q = qk[:, off:off + HEAD_DIM].astype(BF16)
                k = qk[:, D_LSTM + off:D_LSTM + off + HEAD_DIM] * (HEAD_DIM ** -0.5)
                scores.append(lax.dot_general(q, k.astype(BF16), _NT, preferred_element_type=F32))
                parts = [lax.dot_general(q[seg_rows[g]], st_ref[g, hd].astype(BF16), _NT, preferred_element_type=F32)
                         for g in range(seg)]
                inters.append(parts[0] if seg == 1 else jnp.concatenate(parts, axis=0))
                ks.append(k)
            mx.update(ks=ks, scores=scores, inters=inters)

        def m_pool():
            u = mx["u"]
            pos = start + t_m * sl + lax.broadcasted_iota(jnp.int32, (sl, LANES), 0)
            diffs = [[] for _ in POOL_WINDOWS]
            for g in range(seg):
                u_g = u[seg_rows[g]]
                ext = jnp.concatenate([hist_ref[g], u_g], axis=0)
                for i, w in enumerate(POOL_WINDOWS):
                    sl_c = slice(i * POOL_GROUP, (i + 1) * POOL_GROUP)
                    c = ext[:, sl_c]
                    shift = 1
                    while shift < w:
                        c = c + pltpu.roll(c, shift, axis=0)
                        shift *= 2
                    cnt = jnp.minimum(w, pos + 1).astype(F32)
                    diffs[i].append(c[POOL_HIST:] * (1.0 / cnt) - u_g[:, sl_c])
                hist_ref[g] = u_g[sl - POOL_HIST:sl]
            pool_parts = []
            for i in range(len(POOL_WINDOWS)):
                diff = diffs[i][0] if seg == 1 else jnp.concatenate(diffs[i], axis=0)
                pool_parts.append(_dot(diff.astype(BF16), wpool_ref[0, i]))
            mx["pool_out"] = jnp.concatenate(pool_parts, axis=1) * spool_ref[0]

        def m_weights():
            b, d, d_rows, causal = mx["b"], mx["d"], mx["d_rows"], mx["causal"]
            lane_id = lax.broadcasted_iota(jnp.int32, (tt, LANES), 1)
            masked_d, gmax = [], jnp.full((tt, LANES), NEG_BIG, F32)
            for hd in range(N_HEADS):
                md = jnp.where(causal, d_rows[hd:hd + 1, :], NEG_BIG)
                masked_d.append(md)
                gmax = jnp.where(lane_id == hd, jnp.max(md, axis=-1, keepdims=True), gmax)
            m_prev = per_row([m_ref[g:g + 1, :] for g in range(seg)])
            m_t = b + jnp.maximum(m_prev, gmax)
            col_a = b - m_t
            g_inter = jnp.exp(b + m_prev - m_t)
            lasts = [(g + 1) * sl - 1 for g in range(seg)]
            w_last = jnp.exp(per_row([col_a[i:i + 1, :] for i in lasts]) + d)
            for g in range(seg):
                m_ref[g:g + 1, :] = m_t[lasts[g]:lasts[g] + 1, :]
            a_bf, a_sum, wk_bf, wk_sum = [], [], [], []
            for hd in range(N_HEADS):
                a = jnp.exp(col_a[:, hd:hd + 1] + masked_d[hd]) * mx["scores"][hd]
                a_bf.append(a.astype(BF16))
                a_sum.append(jnp.sum(a, axis=-1, keepdims=True))
                wk = w_last[:, hd:hd + 1] * mx["ks"][hd]
                wk_bf.append(wk.astype(BF16))
                wk_sum.append([jnp.sum(wk[seg_rows[g]], axis=0, keepdims=True) for g in range(seg)])
            mx.update(g_inter=g_inter, e_negm=jnp.exp(-m_t), g_last=[g_inter[i:i + 1, :] for i in lasts],
                      a_bf=a_bf, a_sum=a_sum, wk_bf=wk_bf, wk_sum=wk_sum)

        def m_numupd():
            nums, upds = [], []
            for hd in range(N_HEADS):
                v = mx["vo"][:, hd * HEAD_DIM:(hd + 1) * HEAD_DIM].astype(BF16)
                nums.append(_dot(mx["a_bf"][hd], v))
                upds.append([lax.dot_general(v[seg_rows[g]], mx["wk_bf"][hd][seg_rows[g]], _TN,
                                             preferred_element_type=F32) for g in range(seg)])
            mx.update(nums=nums, upds=upds)

        def m_heads():
            head_parts = []
            for hd in range(N_HEADS):
                off = hd * HEAD_DIM
                o = mx["vo"][:, D_LSTM + off:D_LSTM + off + HEAD_DIM]
                inter = mx["inters"][hd]
                gi = mx["g_inter"][:, hd:hd + 1]
                num = mx["nums"][hd] + gi * inter[:, 0:HEAD_DIM]
                den = mx["a_sum"][hd] + gi * inter[:, HEAD_DIM:HEAD_DIM + 1]
                hh = num * (1.0 / jnp.maximum(jnp.abs(den), mx["e_negm"][:, hd:hd + 1]))
                hn = _rms(hh) * ghead_ref[0, :, off:off + HEAD_DIM]
                head_parts.append(hn * _sigmoid(o))
                for g in range(seg):
                    gl = mx["g_last"][g][:, hd:hd + 1]
                    state = st_ref[g, hd]
                    st_ref[g, hd, 0:HEAD_DIM, :] = gl * state[0:HEAD_DIM] + mx["upds"][hd][g]
                    st_ref[g, hd, HEAD_DIM:HEAD_DIM + 1, :] = gl * state[HEAD_DIM:HEAD_DIM + 1] + mx["wk_sum"][hd][g]
            mx["mix_in"] = jnp.concatenate([mx["pool_out"]] + head_parts, axis=1).astype(BF16)

        def m_out():
            x1 = mx["x"] + per_row(mod_rows(modm_ref, 2)) * _dot(mx["mix_in"], wout_ref[0])
            x1_ref[slot_m, rows] = x1
            scale = per_row([g2_ref[0] * (1.0 + sc) for sc in mod_rows(modm_ref, 4)])
            h2_ref[slot_m, rows] = (_rms(x1) * scale + per_row(mod_rows(modm_ref, 3))).astype(BF16)

        mixer_stages = dict(proj=m_proj, gates=m_gates, scores=m_scores, pool=m_pool,
                            weights=m_weights, numupd=m_numupd, heads=m_heads, out=m_out)

        h2 = h2_ref[slot_f, rows]
        stage_rows = 2 * CONV_HIST + sl

        def up_proj(j):
            c0 = j * FF_CHUNK
            return (_dot(h2, wup_ref[0, :, c0:c0 + FF_CHUNK]),
                    _dot(h2, wup_ref[0, :, D_FF + c0:D_FF + c0 + FF_CHUNK]))

        def causal_conv(up, col0, slot, half):
            lanes = slice(half * FF_CHUNK, (half + 1) * FF_CHUNK)
            cols = slice(col0, col0 + FF_CHUNK)
            prev2, prev1 = [], []
            for g in range(seg):
                base = g * stage_rows
                up_g = up[seg_rows[g]]
                hist = cbuf_ref[g, :, cols]
                for k, prev in ((1, prev1), (2, prev2)):
                    ubuf_ref[slot, k - 1, base + k:base + k + CONV_HIST, lanes] = hist
                    ubuf_ref[slot, k - 1, base + k + CONV_HIST:base + k + CONV_HIST + sl, lanes] = up_g
                    prev.append(ubuf_ref[slot, k - 1, base + CONV_HIST:base + CONV_HIST + sl, lanes])
                cbuf_ref[g, :, cols] = up_g[sl - CONV_HIST:sl]
            if seg > 1:
                prev2, prev1 = [jnp.concatenate(prev2, axis=0)], [jnp.concatenate(prev1, axis=0)]
            w = wconv_ref[0, :, cols]
            return bconv_ref[0, :, cols] + w[0:1] * prev2[0] + w[1:2] * prev1[0] + w[2:3] * up

        ups = {j: up_proj(j) for j in range(FF_LOOKAHEAD)}
        for name in MIXER_STAGES_AFTER_CHUNK.get(-1, ()):
            mixer_stages[name]()
        yield
        for j in range(N_FF_CHUNKS):
            if j + FF_LOOKAHEAD < N_FF_CHUNKS:
                ups[j + FF_LOOKAHEAD] = up_proj(j + FF_LOOKAHEAD)
            up_val, up_gate = ups.pop(j)
            val = causal_conv(up_val, j * FF_CHUNK, 2 * r + j % 2, 0)
            gate = causal_conv(up_gate, D_FF + j * FF_CHUNK, 2 * r + j % 2, 1)
            act = ((gate * _sigmoid(gate)) * val).astype(BF16)
            down = _dot(act, wdown_ref[0, j * FF_CHUNK:(j + 1) * FF_CHUNK, :])
            if j == 0:
                acc_ref[r] = down
            else:
                acc_ref[r] += down
            for name in MIXER_STAGES_AFTER_CHUNK.get(j, ()):
                mixer_stages[name]()
            if j + 1 < N_FF_CHUNKS:
                yield
        x2 = x1_ref[slot_f, rows] + per_row(mod_rows(modf_ref, 5)) * acc_ref[r]
        if final:
            x2 = _rms(x2) * gfin_ref[...]
        y_ref[0, rows] = x2

    tiles = [tile_step(r) for r in range(unroll)]
    n_yields = N_FF_CHUNKS
    for r, tile in enumerate(tiles):
        done = TILE_OVERLAP_CHUNKS if r > 0 else 0
        for _ in range(n_yields - done):
            next(tile)
        if r + 1 < unroll:
            for _ in range(TILE_OVERLAP_CHUNKS):
                next(tiles[r + 1])
        for _ in tile:
            pass

    @pl.when((lax.rem(first_m + unroll - 1, nt) == nt - 1) & (s < n_blocks))
    def _store_mixer_state():
        for g in range(seg):
            poolo_ref[g] = hist_ref[g, POOL_HIST - POOL_BUF:POOL_HIST, :]
            co_ref[g] = st_ref[g, :, 0:HEAD_DIM, :]
            for hd in range(N_HEADS):
                no_ref[g, hd:hd + 1, :] = st_ref[g, hd, HEAD_DIM:HEAD_DIM + 1, :]
            mo_ref[g] = m_ref[g:g + 1, 0:N_HEADS]

    @pl.when((lax.rem(first_f + unroll - 1, nt) == nt - 1) & (s >= 1))
    def _store_conv_state():
        for g in range(seg):
            convo_ref[g] = cbuf_ref[g, CONV_HIST - CONV_BUF:CONV_HIST, :]


def _layer_weight(shape, layer):
    nd = len(shape)
    return pl.BlockSpec((1,) + tuple(shape), lambda s: (layer,) + (0,) * nd, pipeline_mode=pl.Buffered(1))


def _run_layer(x, mod, mod_row0, states, prm, g_final, layer, *, seg, sl, unroll, start, final):
    n_seq, t_len, _ = x.shape
    assert n_seq % seg == 0 and t_len % sl == 0 and mod_row0 % seg == 0 and (seg == 1 or t_len == sl)
    assert sl >= POOL_HIST and sl % SUBLANES == 0 and seg <= SUBLANES
    tt = seg * sl
    nt = t_len // sl
    n_groups = n_seq // seg
    assert nt % unroll == 0
    nb = nt // unroll
    n_blocks = n_groups * nb
    carried = states is not None
    xg = x.reshape(n_groups, seg * t_len, D_MODEL)
    mix_b = lambda s: jnp.minimum(s, n_blocks - 1) // nb
    mix_t = lambda s: jnp.minimum(s, n_blocks - 1) % nb
    ffn_b = lambda s: jnp.maximum(s - 1, 0) // nb
    ffn_t = lambda s: jnp.maximum(s - 1, 0) % nb
    mod_g0 = mod_row0 // seg
    in_specs = [
        pl.BlockSpec((1, unroll * tt, D_MODEL), lambda s: (mix_b(s), mix_t(s), 0)),
        pl.BlockSpec((1, seg, 6, D_MODEL), lambda s: (layer, mod_g0 + mix_b(s), 0, 0)),
        pl.BlockSpec((1, seg, 6, D_MODEL), lambda s: (layer, mod_g0 + ffn_b(s), 0, 0)),
    ]
    if carried:
        in_specs += [
            pl.BlockSpec((1, seg, POOL_BUF, D_POOL), lambda s: (layer, mix_b(s), 0, 0)),
            pl.BlockSpec((1, seg, N_HEADS, HEAD_DIM, HEAD_DIM), lambda s: (layer, mix_b(s), 0, 0, 0)),
            pl.BlockSpec((1, seg, N_HEADS, HEAD_DIM), lambda s: (layer, mix_b(s), 0, 0)),
            pl.BlockSpec((1, seg, 1, N_HEADS), lambda s: (layer, mix_b(s), 0, 0)),
            pl.BlockSpec((1, seg, CONV_BUF, 2 * D_FF), lambda s: (layer, ffn_b(s), 0, 0)),
        ]
    in_specs += [
        _layer_weight((1, D_MODEL), layer),
        _layer_weight((D_MODEL, D_MAIN), layer),
        _layer_weight((GATE_ROWS, D_MODEL), layer),
        _layer_weight((GATE_ROWS, LANES), layer),
        _layer_weight((N_POOL_GROUPS, POOL_GROUP, POOL_GROUP), layer),
        _layer_weight((1, D_POOL), layer),
        _layer_weight((1, D_LSTM), layer),
        _layer_weight((D_MODEL, D_MODEL), layer),
        _layer_weight((1, D_MODEL), layer),
        _layer_weight((D_MODEL, 2 * D_FF), layer),
        _layer_weight((CONV_W, 2 * D_FF), layer),
        _layer_weight((1, 2 * D_FF), layer),
        _layer_weight((D_FF, D_MODEL), layer),
        pl.BlockSpec((1, D_MODEL), lambda s: (0, 0), pipeline_mode=pl.Buffered(1)),
    ]
    out_specs = [
        pl.BlockSpec((1, unroll * tt, D_MODEL), lambda s: (ffn_b(s), ffn_t(s), 0)),
        pl.BlockSpec((seg, POOL_BUF, D_POOL), lambda s: (mix_b(s), 0, 0)),
        pl.BlockSpec((seg, N_HEADS, HEAD_DIM, HEAD_DIM), lambda s: (mix_b(s), 0, 0, 0)),
        pl.BlockSpec((seg, N_HEADS, HEAD_DIM), lambda s: (mix_b(s), 0, 0)),
        pl.BlockSpec((seg, 1, N_HEADS), lambda s: (mix_b(s), 0, 0)),
        pl.BlockSpec((seg, CONV_BUF, 2 * D_FF), lambda s: (ffn_b(s), 0, 0)),
    ]
    out_shape = [
        jax.ShapeDtypeStruct((n_groups, seg * t_len, D_MODEL), F32),
        jax.ShapeDtypeStruct((n_seq, POOL_BUF, D_POOL), F32),
        jax.ShapeDtypeStruct((n_seq, N_HEADS, HEAD_DIM, HEAD_DIM), F32),
        jax.ShapeDtypeStruct((n_seq, N_HEADS, HEAD_DIM), F32),
        jax.ShapeDtypeStruct((n_seq, 1, N_HEADS), F32),
        jax.ShapeDtypeStruct((n_seq, CONV_BUF, 2 * D_FF), F32),
    ]
    scratch = [
        pltpu.VMEM((seg, POOL_HIST, D_POOL), F32),
        pltpu.VMEM((seg, N_HEADS, STATE_ROWS, HEAD_DIM), F32),
        pltpu.VMEM((SUBLANES, LANES), F32),
        pltpu.VMEM((seg, CONV_HIST, 2 * D_FF), F32),
        pltpu.VMEM((unroll, tt, D_MODEL), F32),
        pltpu.VMEM((2 * unroll, CONV_W - 1, seg * (2 * CONV_HIST + sl), 2 * FF_CHUNK), F32),
        pltpu.VMEM((2, unroll * tt, D_MODEL), F32),
        pltpu.VMEM((2, unroll * tt, D_MODEL), BF16),
    ]
    kern = functools.partial(_layer_kernel, seg=seg, sl=sl, nt=nt, unroll=unroll, n_blocks=n_blocks, start=start,
                             final=final, carried=carried)
    args = (xg, mod, mod) + (tuple(states) if carried else ()) + tuple(prm) + (g_final,)
    res = pl.pallas_call(
        kern,
        grid=(n_blocks + 1,),
        in_specs=in_specs,
        out_specs=out_specs,
        out_shape=out_shape,
        scratch_shapes=scratch,
        compiler_params=pltpu.CompilerParams(
            dimension_semantics=("arbitrary",),
            vmem_limit_bytes=VMEM_LIMIT_BYTES),
        name="layer_t%d" % tt,
    )(*args)
    return (res[0].reshape(n_seq, t_len, D_MODEL),) + tuple(res[1:])


def _prep_params(g_norm1, w_in, b_gate, w_pool, s_pool, g_head, w_out, g_norm2, w_up, w_conv, b_conv, w_down):
    depth = w_in.shape[0]
    w_in_p = w_in[:, :, :D_MAIN].astype(BF16)
    row_pad = ((0, 0), (0, SUBLANES - N_HEADS), (0, 0))
    wg = jnp.swapaxes(w_in[:, :, D_MAIN:], 1, 2)
    wg_t = jnp.concatenate([jnp.pad(wg[:, :N_HEADS], row_pad), jnp.pad(wg[:, N_HEADS:], row_pad)], axis=1).astype(BF16)
    bg2 = b_gate[:, :, None]
    bg = jnp.broadcast_to(jnp.concatenate([jnp.pad(bg2[:, :N_HEADS], row_pad), jnp.pad(bg2[:, N_HEADS:], row_pad)], axis=1),
                          (depth, GATE_ROWS, LANES))
    return (g_norm1.reshape(depth, 1, D_MODEL), w_in_p, wg_t, bg, w_pool.astype(BF16), s_pool.reshape(depth, 1, D_POOL),
            g_head.reshape(depth, 1, D_LSTM), w_out.astype(BF16), g_norm2.reshape(depth, 1, D_MODEL),
            w_up.astype(BF16), w_conv, b_conv.reshape(depth, 1, 2 * D_FF), w_down.astype(BF16))


def kernel(x_prompt, x_sample, state_pool, state_mlstm_C, state_mlstm_n, state_mlstm_m, state_conv, c_prompt, c_sample, w_ada, b_ada, g_norm1, w_in, b_gate, w_pool, s_pool, g_head, w_out, g_norm2, w_up, w_conv, b_conv, w_down, g_final):
    depth = w_ada.shape[0]
    bp, seq, _ = x_prompt.shape
    bs, dec_seq, _ = x_sample.shape
    tile_rows = MXU_DIM
    seg_s = max(1, min(tile_rows // dec_seq, SUBLANES))
    assert seq % tile_rows == 0 and bs % seg_s == 0 and bp % seg_s == 0

    mod = _adaln_mod(jnp.concatenate([c_prompt, c_sample], axis=0), w_ada, b_ada)
    mod = mod.reshape(depth, bp + bs, 6, D_MODEL)
    gfin = g_final.reshape(1, D_MODEL)
    prm = _prep_params(g_norm1, w_in, b_gate, w_pool, s_pool, g_head, w_out, g_norm2, w_up, w_conv, b_conv, w_down)
    sample_states = (state_pool, state_mlstm_C, state_mlstm_n,
                     state_mlstm_m.reshape(depth, bs, 1, N_HEADS), state_conv)

    xp, xs = x_prompt, x_sample
    outs_p, outs_s = [], []
    for l in range(depth):
        final = l == depth - 1
        res = _run_layer(xp, mod, 0, None, prm, gfin, l, seg=1, sl=tile_rows, unroll=PROMPT_TILES_PER_STEP, start=0,
                         final=final)
        xp = res[0]
        outs_p.append(res[1:])
        res = _run_layer(xs, mod, bp, sample_states, prm, gfin, l, seg=seg_s, sl=dec_seq, unroll=1, start=PAST_LEN,
                         final=final)
        xs = res[0]
        outs_s.append(res[1:])

    def stacked(outs):
        pool, c_mat, n_vec, m_vec, conv = (jnp.stack([o[i] for o in outs]) for i in range(5))
        return pool, c_mat, n_vec, m_vec.reshape(depth, -1, N_HEADS), conv

    return (xp, xs) + stacked(outs_p) + stacked(outs_s)
```

```python
import functools

import jax
import jax.numpy as jnp
from jax import lax
from jax.experimental import pallas as pl
from jax.experimental.pallas import tpu as pltpu

D_MODEL = 1024
D_POOL = 512
N_POOL_GROUPS = 4
POOL_GROUP = 128
POOL_WINDOWS = (2, 4, 8, 16)
POOL_BUF = 15
D_LSTM = 512
N_HEADS = 4
HEAD_DIM = 128
D_MAIN = D_POOL + 4 * D_LSTM
D_FF = 2816
CONV_W = 3
CONV_BUF = 2
EPS = 1e-6
PAST_LEN = 1024

LANES = 128
SUBLANES = 8
MXU_DIM = 256
VMEM_LIMIT_BYTES = 56 * 1024 * 1024

D_INP = D_MAIN + 2 * LANES
POOL_HIST = 2 * SUBLANES
CONV_HIST = SUBLANES
FF_CHUNK = MXU_DIM
N_FF_CHUNKS = D_FF // FF_CHUNK
MIXER_STAGES_AFTER_CHUNK = {-1: ("proj",), 0: ("gates",), 1: ("scores",), 2: ("pool",), 3: ("weights",),
                            4: ("numupd",), 5: ("heads",), 6: ("out",)}
PROMPT_TILES_PER_STEP = 2
TILE_OVERLAP_CHUNKS = 2
FF_LOOKAHEAD = 1
STATE_ROWS = 2 * HEAD_DIM
NEG_BIG = -1e30

_NT = (((1,), (1,)), ((), ()))
_TN = (((0,), (0,)), ((), ()))
F32 = jnp.float32
BF16 = jnp.bfloat16


def _rms(v):
    return v * lax.rsqrt(jnp.mean(v * v, axis=-1, keepdims=True) + EPS)


def _sigmoid(v):
    return 1.0 / (1.0 + jnp.exp(-v))


def _dot(a, b):
    return jnp.dot(a, b, preferred_element_type=F32)


def _mod_kernel(c_ref, w_ref, b_ref, o_ref):
    c = c_ref[...]
    act = (c * _sigmoid(c)).astype(BF16)
    o_ref[0] = _dot(act, w_ref[0].astype(BF16)) + b_ref[0]


def _adaln_mod(c_all, w_ada, b_ada):
    depth = w_ada.shape[0]
    rows = c_all.shape[0]
    n_out = w_ada.shape[2]
    tn = 1536
    return pl.pallas_call(
        _mod_kernel,
        grid=(depth, n_out // tn),
        in_specs=[
            pl.BlockSpec((rows, D_MODEL), lambda l, j: (0, 0)),
            pl.BlockSpec((1, D_MODEL, tn), lambda l, j: (l, 0, j)),
            pl.BlockSpec((1, 1, tn), lambda l, j: (l, 0, j)),
        ],
        out_specs=pl.BlockSpec((1, rows, tn), lambda l, j: (l, 0, j)),
        out_shape=jax.ShapeDtypeStruct((depth, rows, n_out), F32),
        compiler_params=pltpu.CompilerParams(
            dimension_semantics=("arbitrary", "arbitrary"),
            vmem_limit_bytes=VMEM_LIMIT_BYTES),
        name="adaln_mod",
    )(c_all, w_ada, b_ada.reshape(depth, 1, n_out))


def _layer_kernel(*refs, seg, sl, nt, unroll, n_blocks, start, final, carried):
    tt = seg * sl
    if carried:
        x_ref, modm_ref, modf_ref, pool0_ref, c0_ref, n0_ref, m0_ref, conv0_ref = refs[:8]
        refs = refs[8:]
    else:
        x_ref, modm_ref, modf_ref = refs[:3]
        refs = refs[3:]
    (g1_ref, win_ref, bg_ref, wpool_ref, spool_ref, ghead_ref, wout_ref,
     g2_ref, wup_ref, wconv_ref, bconv_ref, wdown_ref, gfin_ref,
     y_ref, poolo_ref, co_ref, no_ref, mo_ref, convo_ref,
     hist_ref, st_ref, m_ref, cbuf_ref, acc_ref, ubuf_ref, x1_ref, h2_ref) = refs
    s = pl.program_id(0)
    first_m = jnp.minimum(s, n_blocks - 1) * unroll
    first_f = jnp.maximum(s - 1, 0) * unroll
    slot_m = lax.rem(s, 2)
    slot_f = 1 - slot_m
    seg_rows = [slice(g * sl, (g + 1) * sl) for g in range(seg)]

    def per_row(vals):
        if seg == 1:
            return vals[0]
        return jnp.concatenate([jnp.broadcast_to(v, (sl, v.shape[-1])) for v in vals], axis=0)

    def mod_rows(mod_ref, r):
        return [mod_ref[0, g, r:r + 1, :] for g in range(seg)]

    split_edges = unroll == 1

    if not split_edges:
        @pl.when(s == 0)
        def _no_previous_tile():
            x1_ref[1] = jnp.zeros((unroll * tt, D_MODEL), F32)
            h2_ref[1] = jnp.zeros((unroll * tt, D_MODEL), BF16)

    @pl.when(lax.rem(first_m, nt) == 0)
    def _load_mixer_state():
        hist_ref[...] = jnp.zeros_like(hist_ref)
        st_ref[...] = jnp.zeros_like(st_ref)
        m_ref[...] = jnp.zeros_like(m_ref)
        if carried:
            for g in range(seg):
                hist_ref[g, POOL_HIST - POOL_BUF:POOL_HIST, :] = pool0_ref[0, g]
                st_ref[g, :, 0:HEAD_DIM, :] = c0_ref[0, g]
                for hd in range(N_HEADS):
                    st_ref[g, hd, HEAD_DIM:HEAD_DIM + 1, :] = n0_ref[0, g, hd:hd + 1, :]
                m_ref[g:g + 1, 0:N_HEADS] = m0_ref[0, g]

    @pl.when(lax.rem(first_f, nt) == 0)
    def _load_conv_state():
        cbuf_ref[...] = jnp.zeros_like(cbuf_ref)
        if carried:
            for g in range(seg):
                cbuf_ref[g, CONV_HIST - CONV_BUF:CONV_HIST, :] = conv0_ref[0, g]

    def tile_step(r, do_mixer, do_ffn):
        rows = slice(r * tt, (r + 1) * tt)
        t_m = lax.rem(first_m + r, nt)
        mx = {}

        def m_proj():
            mx["x"] = x_ref[0, rows]
            scale = per_row([g1_ref[0] * (1.0 + sc) for sc in mod_rows(modm_ref, 1)])
            hb = (_rms(mx["x"]) * scale + per_row(mod_rows(modm_ref, 0))).astype(BF16)
            mx["hb"] = hb
            mx["gates"] = _dot(hb, win_ref[0, :, D_MAIN:D_INP])
            mx["qk"] = _dot(hb, win_ref[0, :, D_POOL:D_POOL + 2 * D_LSTM])

        def m_gates():
            row = lax.broadcasted_iota(jnp.int32, (tt, tt), 0)
            col = lax.broadcasted_iota(jnp.int32, (tt, tt), 1)
            causal = row >= col
            for g in range(1, seg):
                causal = causal & ((row < g * sl) | (col >= g * sl))
            ig = mx["gates"][:, 0:LANES] + bg_ref[0, :, 0:LANES]
            fg = mx["gates"][:, LANES:2 * LANES] + bg_ref[0, :, LANES:2 * LANES]
            lf = jnp.minimum(fg, 0.0) - jnp.log1p(jnp.exp(-jnp.abs(fg)))
            lf_hi = lf.astype(BF16)
            rem = lf - lf_hi.astype(F32)
            lf_mid = rem.astype(BF16)
            lf_lo = (rem - lf_mid.astype(F32)).astype(BF16)
            tri = jnp.where(causal, 1.0, 0.0).astype(BF16)
            b2 = _dot(tri, jnp.concatenate([lf_hi, lf_mid], axis=1))
            b = b2[:, 0:LANES] + b2[:, LANES:2 * LANES] + _dot(tri, lf_lo)
            d = ig - b
            mx.update(causal=causal, b=b, d=d, d_rows=d.T)
            hb = mx["hb"]
            mx["vo"] = _dot(hb, win_ref[0, :, D_POOL + 2 * D_LSTM:D_MAIN])
            mx["u"] = _dot(hb, win_ref[0, :, 0:D_POOL])

        def m_scores():
            ks, scores, inters = [], [], []
            qk = mx["qk"]
            for hd in range(N_HEADS):
                off = hd * HEAD_DIM
                q = qk[:, off:off + HEAD_DIM].astype(BF16)
                k = qk[:, D_LSTM + off:D_LSTM + off + HEAD_DIM] * (HEAD_DIM ** -0.5)
                scores.append(lax.dot_general(q, k.astype(BF16), _NT, preferred_element_type=F32))
                parts = [lax.dot_general(q[seg_rows[g]], st_ref[g, hd].astype(BF16), _NT, preferred_element_type=F32)
                         for g in range(seg)]
                inters.append(parts[0] if seg == 1 else jnp.concatenate(parts, axis=0))
                ks.append(k)
            mx.update(ks=ks, scores=scores, inters=inters)

        def m_pool():
            u = mx["u"]
            pos = start + t_m * sl + lax.broadcasted_iota(jnp.int32, (sl, LANES), 0)
            diffs = [[] for _ in POOL_WINDOWS]
            for g in range(seg):
                u_g = u[seg_rows[g]]
                ext = jnp.concatenate([hist_ref[g], u_g], axis=0)
                for i, w in enumerate(POOL_WINDOWS):
                    sl_c = slice(i * POOL_GROUP, (i + 1) * POOL_GROUP)
                    c = ext[:, sl_c]
                    shift = 1
                    while shift < w:
                        c = c + pltpu.roll(c, shift, axis=0)
                        shift *= 2
                    cnt = jnp.minimum(w, pos + 1).astype(F32)
                    diffs[i].append(c[POOL_HIST:] * (1.0 / cnt) - u_g[:, sl_c])
                hist_ref[g] = u_g[sl - POOL_HIST:sl]
            pool_parts = []
            for i in range(len(POOL_WINDOWS)):
                diff = diffs[i][0] if seg == 1 else jnp.concatenate(diffs[i], axis=0)
                pool_parts.append(_dot(diff.astype(BF16), wpool_ref[0, i]))
            mx["pool_out"] = jnp.concatenate(pool_parts, axis=1) * spool_ref[0]

        def m_weights():
            b, d, d_rows, causal = mx["b"], mx["d"], mx["d_rows"], mx["causal"]
            lane_id = lax.broadcasted_iota(jnp.int32, (tt, LANES), 1)
            masked_d, gmax = [], jnp.full((tt, LANES), NEG_BIG, F32)
            for hd in range(N_HEADS):
                md = jnp.where(causal, d_rows[hd:hd + 1, :], NEG_BIG)
                masked_d.append(md)
                gmax = jnp.where(lane_id == hd, jnp.max(md, axis=-1, keepdims=True), gmax)
            m_prev = per_row([m_ref[g:g + 1, :] for g in range(seg)])
            m_t = b + jnp.maximum(m_prev, gmax)
            col_a = b - m_t
            g_inter = jnp.exp(b + m_prev - m_t)
            lasts = [(g + 1) * sl - 1 for g in range(seg)]
            w_last = jnp.exp(per_row([col_a[i:i + 1, :] for i in lasts]) + d)
            for g in range(seg):
                m_ref[g:g + 1, :] = m_t[lasts[g]:lasts[g] + 1, :]
            a_bf, a_sum, wk_bf, wk_sum = [], [], [], []
            for hd in range(N_HEADS):
                a = jnp.exp(col_a[:, hd:hd + 1] + masked_d[hd]) * mx["scores"][hd]
                a_bf.append(a.astype(BF16))
                a_sum.append(jnp.sum(a, axis=-1, keepdims=True))
                wk = w_last[:, hd:hd + 1] * mx["ks"][hd]
                wk_bf.append(wk.astype(BF16))
                wk_sum.append([jnp.sum(wk[seg_rows[g]], axis=0, keepdims=True) for g in range(seg)])
            mx.update(g_inter=g_inter, e_negm=jnp.exp(-m_t), g_last=[g_inter[i:i + 1, :] for i in lasts],
                      a_bf=a_bf, a_sum=a_sum, wk_bf=wk_bf, wk_sum=wk_sum)

        def m_numupd():
            nums, upds = [], []
            for hd in range(N_HEADS):
                v = mx["vo"][:, hd * HEAD_DIM:(hd + 1) * HEAD_DIM].astype(BF16)
                nums.append(_dot(mx["a_bf"][hd], v))
                upds.append([lax.dot_general(v[seg_rows[g]], mx["wk_bf"][hd][seg_rows[g]], _TN,
                                             preferred_element_type=F32) for g in range(seg)])
            mx.update(nums=nums, upds=upds)

        def m_heads():
            head_parts = []
            for hd in range(N_HEADS):
                off = hd * HEAD_DIM
                o = mx["vo"][:, D_LSTM + off:D_LSTM + off + HEAD_DIM]
                inter = mx["inters"][hd]
                gi = mx["g_inter"][:, hd:hd + 1]
                num = mx["nums"][hd] + gi * inter[:, 0:HEAD_DIM]
                den = mx["a_sum"][hd] + gi * inter[:, HEAD_DIM:HEAD_DIM + 1]
                hh = num * (1.0 / jnp.maximum(jnp.abs(den), mx["e_negm"][:, hd:hd + 1]))
                hn = _rms(hh) * ghead_ref[0, :, off:off + HEAD_DIM]
                head_parts.append(hn * _sigmoid(o))
                for g in range(seg):
                    gl = mx["g_last"][g][:, hd:hd + 1]
                    state = st_ref[g, hd]
                    st_ref[g, hd, 0:HEAD_DIM, :] = gl * state[0:HEAD_DIM] + mx["upds"][hd][g]
                    st_ref[g, hd, HEAD_DIM:HEAD_DIM + 1, :] = gl * state[HEAD_DIM:HEAD_DIM + 1] + mx["wk_sum"][hd][g]
            mx["mix_in"] = jnp.concatenate([mx["pool_out"]] + head_parts, axis=1).astype(BF16)

        def m_out():
            x1 = mx["x"] + per_row(mod_rows(modm_ref, 2)) * _dot(mx["mix_in"], wout_ref[0])
            x1_ref[slot_m, rows] = x1
            scale = per_row([g2_ref[0] * (1.0 + sc) for sc in mod_rows(modm_ref, 4)])
            h2_ref[slot_m, rows] = (_rms(x1) * scale + per_row(mod_rows(modm_ref, 3))).astype(BF16)

        mixer_stages = dict(proj=m_proj, gates=m_gates, scores=m_scores, pool=m_pool,
                            weights=m_weights, numupd=m_numupd, heads=m_heads, out=m_out)

        h2 = h2_ref[slot_f, rows] if do_ffn else None
        stage_rows = 2 * CONV_HIST + sl

        def up_proj(j):
            c0 = j * FF_CHUNK
            return (_dot(h2, wup_ref[0, :, c0:c0 + FF_CHUNK]),
                    _dot(h2, wup_ref[0, :, D_FF + c0:D_FF + c0 + FF_CHUNK]))

        def causal_conv(up, col0, slot, half):
            lanes = slice(half * FF_CHUNK, (half + 1) * FF_CHUNK)
            cols = slice(col0, col0 + FF_CHUNK)
            prev2, prev1 = [], []
            for g in range(seg):
                base = g * stage_rows
                up_g = up[seg_rows[g]]
                hist = cbuf_ref[g, :, cols]
                for k, prev in ((1, prev1), (2, prev2)):
                    ubuf_ref[slot, k - 1, base + k:base + k + CONV_HIST, lanes] = hist
                    ubuf_ref[slot, k - 1, base + k + CONV_HIST:base + k + CONV_HIST + sl, lanes] = up_g
                    prev.append(ubuf_ref[slot, k - 1, base + CONV_HIST:base + CONV_HIST + sl, lanes])
                cbuf_ref[g, :, cols] = up_g[sl - CONV_HIST:sl]
            if seg > 1:
                prev2, prev1 = [jnp.concatenate(prev2, axis=0)], [jnp.concatenate(prev1, axis=0)]
            w = wconv_ref[0, :, cols]
            return bconv_ref[0, :, cols] + w[0:1] * prev2[0] + w[1:2] * prev1[0] + w[2:3] * up

        def run_stages(after_chunk):
            if do_mixer:
                for name in MIXER_STAGES_AFTER_CHUNK.get(after_chunk, ()):
                    mixer_stages[name]()

        ups = {j: up_proj(j) for j in range(FF_LOOKAHEAD)} if do_ffn else {}
        run_stages(-1)
        yield
        for j in range(N_FF_CHUNKS):
            if do_ffn:
                if j + FF_LOOKAHEAD < N_FF_CHUNKS:
                    ups[j + FF_LOOKAHEAD] = up_proj(j + FF_LOOKAHEAD)
                up_val, up_gate = ups.pop(j)
                val = causal_conv(up_val, j * FF_CHUNK, 2 * r + j % 2, 0)
                gate = causal_conv(up_gate, D_FF + j * FF_CHUNK, 2 * r + j % 2, 1)
                act = ((gate * _sigmoid(gate)) * val).astype(BF16)
                down = _dot(act, wdown_ref[0, j * FF_CHUNK:(j + 1) * FF_CHUNK, :])
                if j == 0:
                    acc_ref[r] = down
                else:
                    acc_ref[r] += down
            run_stages(j)
            if j + 1 < N_FF_CHUNKS:
                yield
        if do_ffn:
            x2 = x1_ref[slot_f, rows] + per_row(mod_rows(modf_ref, 5)) * acc_ref[r]
            if final:
                x2 = _rms(x2) * gfin_ref[...]
            y_ref[0, rows] = x2

    def run_block(do_mixer, do_ffn):
        tiles = [tile_step(r, do_mixer, do_ffn) for r in range(unroll)]
        n_yields = N_FF_CHUNKS
        for r, tile in enumerate(tiles):
            done = TILE_OVERLAP_CHUNKS if r > 0 else 0
            for _ in range(n_yields - done):
                next(tile)
            if r + 1 < unroll:
                for _ in range(TILE_OVERLAP_CHUNKS):
                    next(tiles[r + 1])
            for _ in tile:
                pass

    if split_edges:
        lax.cond(s == 0, lambda: run_block(True, False),
                 lambda: lax.cond(s == n_blocks, lambda: run_block(False, True), lambda: run_block(True, True)))
    else:
        run_block(True, True)

    @pl.when((lax.rem(first_m + unroll - 1, nt) == nt - 1) & (s < n_blocks))
    def _store_mixer_state():
        for g in range(seg):
            poolo_ref[g] = hist_ref[g, POOL_HIST - POOL_BUF:POOL_HIST, :]
            co_ref[g] = st_ref[g, :, 0:HEAD_DIM, :]
            for hd in range(N_HEADS):
                no_ref[g, hd:hd + 1, :] = st_ref[g, hd, HEAD_DIM:HEAD_DIM + 1, :]
            mo_ref[g] = m_ref[g:g + 1, 0:N_HEADS]

    @pl.when((lax.rem(first_f + unroll - 1, nt) == nt - 1) & (s >= 1))
    def _store_conv_state():
        for g in range(seg):
            convo_ref[g] = cbuf_ref[g, CONV_HIST - CONV_BUF:CONV_HIST, :]


def _layer_weight(shape, layer):
    nd = len(shape)
    return pl.BlockSpec((1,) + tuple(shape), lambda s: (layer,) + (0,) * nd, pipeline_mode=pl.Buffered(1))


def _run_layer(x, mod, mod_row0, states, prm, g_final, layer, *, seg, sl, unroll, start, final):
    n_seq, t_len, _ = x.shape
    assert n_seq % seg == 0 and t_len % sl == 0 and mod_row0 % seg == 0 and (seg == 1 or t_len == sl)
    assert sl >= POOL_HIST and sl % SUBLANES == 0 and seg <= SUBLANES
    tt = seg * sl
    nt = t_len // sl
    n_groups = n_seq // seg
    assert nt % unroll == 0
    nb = nt // unroll
    n_blocks = n_groups * nb
    carried = states is not None
    xg = x.reshape(n_groups, seg * t_len, D_MODEL)
    mix_b = lambda s: jnp.minimum(s, n_blocks - 1) // nb
    mix_t = lambda s: jnp.minimum(s, n_blocks - 1) % nb
    ffn_b = lambda s: jnp.maximum(s - 1, 0) // nb
    ffn_t = lambda s: jnp.maximum(s - 1, 0) % nb
    mod_g0 = mod_row0 // seg
    in_specs = [
        pl.BlockSpec((1, unroll * tt, D_MODEL), lambda s: (mix_b(s), mix_t(s), 0)),
        pl.BlockSpec((1, seg, 6, D_MODEL), lambda s: (layer, mod_g0 + mix_b(s), 0, 0)),
        pl.BlockSpec((1, seg, 6, D_MODEL), lambda s: (layer, mod_g0 + ffn_b(s), 0, 0)),
    ]
    if carried:
        in_specs += [
            pl.BlockSpec((1, seg, POOL_BUF, D_POOL), lambda s: (layer, mix_b(s), 0, 0)),
            pl.BlockSpec((1, seg, N_HEADS, HEAD_DIM, HEAD_DIM), lambda s: (layer, mix_b(s), 0, 0, 0)),
            pl.BlockSpec((1, seg, N_HEADS, HEAD_DIM), lambda s: (layer, mix_b(s), 0, 0)),
            pl.BlockSpec((1, seg, 1, N_HEADS), lambda s: (layer, mix_b(s), 0, 0)),
            pl.BlockSpec((1, seg, CONV_BUF, 2 * D_FF), lambda s: (layer, ffn_b(s), 0, 0)),
        ]
    in_specs += [
        _layer_weight((1, D_MODEL), layer),
        _layer_weight((D_MODEL, D_INP), layer),
        _layer_weight((1, 2 * LANES), layer),
        _layer_weight((N_POOL_GROUPS, POOL_GROUP, POOL_GROUP), layer),
        _layer_weight((1, D_POOL), layer),
        _layer_weight((1, D_LSTM), layer),
        _layer_weight((D_MODEL, D_MODEL), layer),
        _layer_weight((1, D_MODEL), layer),
        _layer_weight((D_MODEL, 2 * D_FF), layer),
        _layer_weight((CONV_W, 2 * D_FF), layer),
        _layer_weight((1, 2 * D_FF), layer),
        _layer_weight((D_FF, D_MODEL), layer),
        pl.BlockSpec((1, D_MODEL), lambda s: (0, 0), pipeline_mode=pl.Buffered(1)),
    ]
    out_specs = [
        pl.BlockSpec((1, unroll * tt, D_MODEL), lambda s: (ffn_b(s), ffn_t(s), 0)),
        pl.BlockSpec((seg, POOL_BUF, D_POOL), lambda s: (mix_b(s), 0, 0)),
        pl.BlockSpec((seg, N_HEADS, HEAD_DIM, HEAD_DIM), lambda s: (mix_b(s), 0, 0, 0)),
        pl.BlockSpec((seg, N_HEADS, HEAD_DIM), lambda s: (mix_b(s), 0, 0)),
        pl.BlockSpec((seg, 1, N_HEADS), lambda s: (mix_b(s), 0, 0)),
        pl.BlockSpec((seg, CONV_BUF, 2 * D_FF), lambda s: (ffn_b(s), 0, 0)),
    ]
    out_shape = [
        jax.ShapeDtypeStruct((n_groups, seg * t_len, D_MODEL), F32),
        jax.ShapeDtypeStruct((n_seq, POOL_BUF, D_POOL), F32),
        jax.ShapeDtypeStruct((n_seq, N_HEADS, HEAD_DIM, HEAD_DIM), F32),
        jax.ShapeDtypeStruct((n_seq, N_HEADS, HEAD_DIM), F32),
        jax.ShapeDtypeStruct((n_seq, 1, N_HEADS), F32),
        jax.ShapeDtypeStruct((n_seq, CONV_BUF, 2 * D_FF), F32),
    ]
    scratch = [
        pltpu.VMEM((seg, POOL_HIST, D_POOL), F32),
        pltpu.VMEM((seg, N_HEADS, STATE_ROWS, HEAD_DIM), F32),
        pltpu.VMEM((SUBLANES, LANES), F32),
        pltpu.VMEM((seg, CONV_HIST, 2 * D_FF), F32),
        pltpu.VMEM((unroll, tt, D_MODEL), F32),
        pltpu.VMEM((2 * unroll, CONV_W - 1, seg * (2 * CONV_HIST + sl), 2 * FF_CHUNK), F32),
        pltpu.VMEM((2, unroll * tt, D_MODEL), F32),
        pltpu.VMEM((2, unroll * tt, D_MODEL), BF16),
    ]
    kern = functools.partial(_layer_kernel, seg=seg, sl=sl, nt=nt, unroll=unroll, n_blocks=n_blocks, start=start,
                             final=final, carried=carried)
    args = (xg, mod, mod) + (tuple(states) if carried else ()) + tuple(prm) + (g_final,)
    res = pl.pallas_call(
        kern,
        grid=(n_blocks + 1,),
        in_specs=in_specs,
        out_specs=out_specs,
        out_shape=out_shape,
        scratch_shapes=scratch,
        compiler_params=pltpu.CompilerParams(
            dimension_semantics=("arbitrary",),
            vmem_limit_bytes=VMEM_LIMIT_BYTES),
        name="layer_t%d" % tt,
    )(*args)
    return (res[0].reshape(n_seq, t_len, D_MODEL),) + tuple(res[1:])


def _prep_params(g_norm1, w_in, b_gate, w_pool, s_pool, g_head, w_out, g_norm2, w_up, w_conv, b_conv, w_down):
    depth = w_in.shape[0]
    lane_pad = ((0, 0), (0, 0), (0, LANES - N_HEADS))
    w_in_p = jnp.concatenate(
        [w_in[:, :, :D_MAIN], jnp.pad(w_in[:, :, D_MAIN:D_MAIN + N_HEADS], lane_pad),
         jnp.pad(w_in[:, :, D_MAIN + N_HEADS:], lane_pad)], axis=2).astype(BF16)
    bg3 = b_gate[:, None, :]
    bg = jnp.concatenate([jnp.pad(bg3[:, :, :N_HEADS], lane_pad), jnp.pad(bg3[:, :, N_HEADS:], lane_pad)], axis=2)
    return (g_norm1.reshape(depth, 1, D_MODEL), w_in_p, bg, w_pool.astype(BF16), s_pool.reshape(depth, 1, D_POOL),
            g_head.reshape(depth, 1, D_LSTM), w_out.astype(BF16), g_norm2.reshape(depth, 1, D_MODEL),
            w_up.astype(BF16), w_conv, b_conv.reshape(depth, 1, 2 * D_FF), w_down.astype(BF16))


def kernel(x_prompt, x_sample, state_pool, state_mlstm_C, state_mlstm_n, state_mlstm_m, state_conv, c_prompt, c_sample, w_ada, b_ada, g_norm1, w_in, b_gate, w_pool, s_pool, g_head, w_out, g_norm2, w_up, w_conv, b_conv, w_down, g_final):
    depth = w_ada.shape[0]
    bp, seq, _ = x_prompt.shape
    bs, dec_seq, _ = x_sample.shape
    tile_rows = MXU_DIM
    seg_s = max(1, min(tile_rows // dec_seq, SUBLANES))
    assert seq % tile_rows == 0 and bs % seg_s == 0 and bp % seg_s == 0

    mod = _adaln_mod(jnp.concatenate([c_prompt, c_sample], axis=0), w_ada, b_ada)
    mod = mod.reshape(depth, bp + bs, 6, D_MODEL)
    gfin = g_final.reshape(1, D_MODEL)
    prm = _prep_params(g_norm1, w_in, b_gate, w_pool, s_pool, g_head, w_out, g_norm2, w_up, w_conv, b_conv, w_down)
    sample_states = (state_pool, state_mlstm_C, state_mlstm_n,
                     state_mlstm_m.reshape(depth, bs, 1, N_HEADS), state_conv)

    xp, xs = x_prompt, x_sample
    outs_p, outs_s = [], []
    for l in range(depth):
        final = l == depth - 1
        res = _run_layer(xp, mod, 0, None, prm, gfin, l, seg=1, sl=tile_rows, unroll=PROMPT_TILES_PER_STEP, start=0,
                         final=final)
        xp = res[0]
        outs_p.append(res[1:])
        res = _run_layer(xs, mod, bp, sample_states, prm, gfin, l, seg=seg_s, sl=dec_seq, unroll=1, start=PAST_LEN,
                         final=final)
        xs = res[0]
        outs_s.append(res[1:])

    def stacked(outs):
        pool, c_mat, n_vec, m_vec, conv = (jnp.stack([o[i] for o in outs]) for i in range(5))
        return pool, c_mat, n_vec, m_vec.reshape(depth, -1, N_HEADS), conv

    return (xp, xs) + stacked(outs_p) + stacked(outs_s)
```

```python
import functools

import jax
import jax.numpy as jnp
from jax import lax
from jax.experimental import pallas as pl
from jax.experimental.pallas import tpu as pltpu

D_MODEL = 1024
D_POOL = 512
N_POOL_GROUPS = 4
POOL_GROUP = 128
POOL_WINDOWS = (2, 4, 8, 16)
POOL_BUF = 15
D_LSTM = 512
N_HEADS = 4
HEAD_DIM = 128
D_MAIN = D_POOL + 4 * D_LSTM
D_FF = 2816
CONV_W = 3
CONV_BUF = 2
EPS = 1e-6
PAST_LEN = 1024

LANES = 128
SUBLANES = 8
MXU_DIM = 256
VMEM_LIMIT_BYTES = 56 * 1024 * 1024

D_INP = D_MAIN + 2 * LANES
POOL_HIST = 2 * SUBLANES
CONV_HIST = SUBLANES
FF_CHUNK = MXU_DIM
N_FF_CHUNKS = D_FF // FF_CHUNK
MIXER_STAGES_AFTER_CHUNK = {-1: ("proj",), 0: ("gates",), 1: ("scores",), 2: ("pool",), 3: ("weights",),
                            4: ("numupd",), 5: ("heads",), 6: ("out",)}
PROMPT_TILES_PER_STEP = 2
TILE_OVERLAP_CHUNKS = 2
FF_LOOKAHEAD = 1
STATE_ROWS = 2 * HEAD_DIM
NEG_BIG = -1e30

_NT = (((1,), (1,)), ((), ()))
_TN = (((0,), (0,)), ((), ()))
F32 = jnp.float32
BF16 = jnp.bfloat16


def _rms(v):
    return v * lax.rsqrt(jnp.mean(v * v, axis=-1, keepdims=True) + EPS)


def _sigmoid(v):
    return 1.0 / (1.0 + jnp.exp(-v))


def _dot(a, b):
    return jnp.dot(a, b, preferred_element_type=F32)


def _mod_kernel(c_ref, w_ref, b_ref, o_ref):
    c = c_ref[...]
    act = (c * _sigmoid(c)).astype(BF16)
    o_ref[0] = _dot(act, w_ref[0].astype(BF16)) + b_ref[0]


def _adaln_mod(c_all, w_ada, b_ada):
    depth = w_ada.shape[0]
    rows = c_all.shape[0]
    n_out = w_ada.shape[2]
    tn = 1536
    return pl.pallas_call(
        _mod_kernel,
        grid=(depth, n_out // tn),
        in_specs=[
            pl.BlockSpec((rows, D_MODEL), lambda l, j: (0, 0)),
            pl.BlockSpec((1, D_MODEL, tn), lambda l, j: (l, 0, j)),
            pl.BlockSpec((1, 1, tn), lambda l, j: (l, 0, j)),
        ],
        out_specs=pl.BlockSpec((1, rows, tn), lambda l, j: (l, 0, j)),
        out_shape=jax.ShapeDtypeStruct((depth, rows, n_out), F32),
        compiler_params=pltpu.CompilerParams(
            dimension_semantics=("arbitrary", "arbitrary"),
            vmem_limit_bytes=VMEM_LIMIT_BYTES),
        name="adaln_mod",
    )(c_all, w_ada, b_ada.reshape(depth, 1, n_out))


def _layer_kernel(*refs, seg, sl, nt, unroll, n_blocks, start, final, carried):
    tt = seg * sl
    if carried:
        x_ref, modm_ref, modf_ref, pool0_ref, c0_ref, n0_ref, m0_ref, conv0_ref = refs[:8]
        refs = refs[8:]
    else:
        x_ref, modm_ref, modf_ref = refs[:3]
        refs = refs[3:]
    (g1_ref, win_ref, bg_ref, wpool_ref, spool_ref, ghead_ref, wout_ref,
     g2_ref, wup_ref, wconv_ref, bconv_ref, wdown_ref, gfin_ref,
     y_ref, poolo_ref, co_ref, no_ref, mo_ref, convo_ref,
     hist_ref, st_ref, m_ref, cbuf_ref, acc_ref, ubuf_ref, x1_ref, h2_ref) = refs
    s = pl.program_id(0)
    first_m = jnp.minimum(s, n_blocks - 1) * unroll
    first_f = jnp.maximum(s - 1, 0) * unroll
    slot_m = lax.rem(s, 2)
    slot_f = 1 - slot_m
    seg_rows = [slice(g * sl, (g + 1) * sl) for g in range(seg)]

    def per_row(vals):
        if seg == 1:
            return vals[0]
        return jnp.concatenate([jnp.broadcast_to(v, (sl, v.shape[-1])) for v in vals], axis=0)

    def mod_rows(mod_ref, r):
        return [mod_ref[0, g, r:r + 1, :] for g in range(seg)]

    @pl.when(s == 0)
    def _no_previous_tile():
        x1_ref[1] = jnp.zeros((unroll * tt, D_MODEL), F32)
        h2_ref[1] = jnp.zeros((unroll * tt, D_MODEL), BF16)

    @pl.when(lax.rem(first_m, nt) == 0)
    def _load_mixer_state():
        hist_ref[...] = jnp.zeros_like(hist_ref)
        st_ref[...] = jnp.zeros_like(st_ref)
        m_ref[...] = jnp.zeros_like(m_ref)
        if carried:
            for g in range(seg):
                hist_ref[g, POOL_HIST - POOL_BUF:POOL_HIST, :] = pool0_ref[0, g]
                st_ref[g, :, 0:HEAD_DIM, :] = c0_ref[0, g]
                for hd in range(N_HEADS):
                    st_ref[g, hd, HEAD_DIM:HEAD_DIM + 1, :] = n0_ref[0, g, hd:hd + 1, :]
                m_ref[g:g + 1, 0:N_HEADS] = m0_ref[0, g]

    @pl.when(lax.rem(first_f, nt) == 0)
    def _load_conv_state():
        cbuf_ref[...] = jnp.zeros_like(cbuf_ref)
        if carried:
            for g in range(seg):
                cbuf_ref[g, CONV_HIST - CONV_BUF:CONV_HIST, :] = conv0_ref[0, g]

    def tile_step(r):
        rows = slice(r * tt, (r + 1) * tt)
        t_m = lax.rem(first_m + r, nt)
        mx = {}

        def m_proj():
            scale = per_row([g1_ref[0] * (1.0 + sc) for sc in mod_rows(modm_ref, 1)])
            hb = (_rms(x_ref[0, rows]) * scale + per_row(mod_rows(modm_ref, 0))).astype(BF16)
            mx["hb"] = hb
            mx["gates"] = _dot(hb, win_ref[0, :, D_MAIN:D_INP])
            mx["qk"] = _dot(hb, win_ref[0, :, D_POOL:D_POOL + 2 * D_LSTM])

        def m_gates():
            row = lax.broadcasted_iota(jnp.int32, (tt, tt), 0)
            col = lax.broadcasted_iota(jnp.int32, (tt, tt), 1)
            causal = row >= col
            for g in range(1, seg):
                causal = causal & ((row < g * sl) | (col >= g * sl))
            ig = mx["gates"][:, 0:LANES] + bg_ref[0, :, 0:LANES]
            fg = mx["gates"][:, LANES:2 * LANES] + bg_ref[0, :, LANES:2 * LANES]
            lf = jnp.minimum(fg, 0.0) - jnp.log1p(jnp.exp(-jnp.abs(fg)))
            lf_hi = lf.astype(BF16)
            rem = lf - lf_hi.astype(F32)
            lf_mid = rem.astype(BF16)
            lf_lo = (rem - lf_mid.astype(F32)).astype(BF16)
            tri = jnp.where(causal, 1.0, 0.0).astype(BF16)
            b2 = _dot(tri, jnp.concatenate([lf_hi, lf_mid], axis=1))
            b = b2[:, 0:LANES] + b2[:, LANES:2 * LANES] + _dot(tri, lf_lo)
            d = ig - b
            mx.update(causal=causal, b=b, d=d, d_rows=d.T)
            hb = mx["hb"]
            mx["vo"] = _dot(hb, win_ref[0, :, D_POOL + 2 * D_LSTM:D_MAIN])
            mx["u"] = _dot(hb, win_ref[0, :, 0:D_POOL])

        def m_scores():
            ks, scores, inters = [], [], []
            qk = mx["qk"]
            for hd in range(N_HEADS):
                off = hd * HEAD_DIM
                q = qk[:, off:off + HEAD_DIM].astype(BF16)
                k = qk[:, D_LSTM + off:D_LSTM + off + HEAD_DIM] * (HEAD_DIM ** -0.5)
                scores.append(lax.dot_general(q, k.astype(BF16), _NT, preferred_element_type=F32))
                parts = [lax.dot_general(q[seg_rows[g]], st_ref[g, hd].astype(BF16), _NT, preferred_element_type=F32)
                         for g in range(seg)]
                inters.append(parts[0] if seg == 1 else jnp.concatenate(parts, axis=0))
                ks.append(k)
            mx.update(ks=ks, scores=scores, inters=inters)

        def m_pool():
            u = mx["u"]
            pos = start + t_m * sl + lax.broadcasted_iota(jnp.int32, (sl, LANES), 0)
            diffs = [[] for _ in POOL_WINDOWS]
            for g in range(seg):
                u_g = u[seg_rows[g]]
                ext = jnp.concatenate([hist_ref[g], u_g], axis=0)
                for i, w in enumerate(POOL_WINDOWS):
                    sl_c = slice(i * POOL_GROUP, (i + 1) * POOL_GROUP)
                    c = ext[:, sl_c]
                    shift = 1
                    while shift < w:
                        c = c + pltpu.roll(c, shift, axis=0)
                        shift *= 2
                    cnt = jnp.minimum(w, pos + 1).astype(F32)
                    diffs[i].append(c[POOL_HIST:] * (1.0 / cnt) - u_g[:, sl_c])
                hist_ref[g] = u_g[sl - POOL_HIST:sl]
            pool_parts = []
            for i in range(len(POOL_WINDOWS)):
                diff = diffs[i][0] if seg == 1 else jnp.concatenate(diffs[i], axis=0)
                pool_parts.append(_dot(diff.astype(BF16), wpool_ref[0, i]))
            mx["pool_out"] = jnp.concatenate(pool_parts, axis=1) * spool_ref[0]

        def m_weights():
            b, d, d_rows, causal = mx["b"], mx["d"], mx["d_rows"], mx["causal"]
            lane_id = lax.broadcasted_iota(jnp.int32, (tt, LANES), 1)
            masked_d, gmax = [], jnp.full((tt, LANES), NEG_BIG, F32)
            for hd in range(N_HEADS):
                md = jnp.where(causal, d_rows[hd:hd + 1, :], NEG_BIG)
                masked_d.append(md)
                gmax = jnp.where(lane_id == hd, jnp.max(md, axis=-1, keepdims=True), gmax)
            m_prev = per_row([m_ref[g:g + 1, :] for g in range(seg)])
            m_t = b + jnp.maximum(m_prev, gmax)
            col_a = b - m_t
            g_inter = jnp.exp(b + m_prev - m_t)
            lasts = [(g + 1) * sl - 1 for g in range(seg)]
            w_last = jnp.exp(per_row([col_a[i:i + 1, :] for i in lasts]) + d)
            for g in range(seg):
                m_ref[g:g + 1, :] = m_t[lasts[g]:lasts[g] + 1, :]
            a_bf, a_sum, wk_bf, wk_sum = [], [], [], []
            for hd in range(N_HEADS):
                a = jnp.exp(col_a[:, hd:hd + 1] + masked_d[hd]) * mx["scores"][hd]
                a_bf.append(a.astype(BF16))
                a_sum.append(jnp.sum(a, axis=-1, keepdims=True))
                wk = w_last[:, hd:hd + 1] * mx["ks"][hd]
                wk_bf.append(wk.astype(BF16))
                wk_sum.append([jnp.sum(wk[seg_rows[g]], axis=0, keepdims=True) for g in range(seg)])
            mx.update(g_inter=g_inter, e_negm=jnp.exp(-m_t), g_last=[g_inter[i:i + 1, :] for i in lasts],
                      a_bf=a_bf, a_sum=a_sum, wk_bf=wk_bf, wk_sum=wk_sum)

        def m_numupd():
            nums, upds = [], []
            for hd in range(N_HEADS):
                v = mx["vo"][:, hd * HEAD_DIM:(hd + 1) * HEAD_DIM].astype(BF16)
                nums.append(_dot(mx["a_bf"][hd], v))
                upds.append([lax.dot_general(v[seg_rows[g]], mx["wk_bf"][hd][seg_rows[g]], _TN,
                                             preferred_element_type=F32) for g in range(seg)])
            mx.update(nums=nums, upds=upds)

        def m_heads():
            head_parts = []
            for hd in range(N_HEADS):
                off = hd * HEAD_DIM
                o = mx["vo"][:, D_LSTM + off:D_LSTM + off + HEAD_DIM]
                inter = mx["inters"][hd]
                gi = mx["g_inter"][:, hd:hd + 1]
                num = mx["nums"][hd] + gi * inter[:, 0:HEAD_DIM]
                den = mx["a_sum"][hd] + gi * inter[:, HEAD_DIM:HEAD_DIM + 1]
                hh = num * (1.0 / jnp.maximum(jnp.abs(den), mx["e_negm"][:, hd:hd + 1]))
                hn = _rms(hh) * ghead_ref[0, :, off:off + HEAD_DIM]
                head_parts.append(hn * _sigmoid(o))
                for g in range(seg):
                    gl = mx["g_last"][g][:, hd:hd + 1]
                    state = st_ref[g, hd]
                    st_ref[g, hd, 0:HEAD_DIM, :] = gl * state[0:HEAD_DIM] + mx["upds"][hd][g]
                    st_ref[g, hd, HEAD_DIM:HEAD_DIM + 1, :] = gl * state[HEAD_DIM:HEAD_DIM + 1] + mx["wk_sum"][hd][g]
            mx["mix_in"] = jnp.concatenate([mx["pool_out"]] + head_parts, axis=1).astype(BF16)

        def m_out():
            x1 = x_ref[0, rows] + per_row(mod_rows(modm_ref, 2)) * _dot(mx["mix_in"], wout_ref[0])
            x1_ref[slot_m, rows] = x1
            scale = per_row([g2_ref[0] * (1.0 + sc) for sc in mod_rows(modm_ref, 4)])
            h2_ref[slot_m, rows] = (_rms(x1) * scale + per_row(mod_rows(modm_ref, 3))).astype(BF16)

        mixer_stages = dict(proj=m_proj, gates=m_gates, scores=m_scores, pool=m_pool,
                            weights=m_weights, numupd=m_numupd, heads=m_heads, out=m_out)

        stage_rows = 2 * CONV_HIST + sl

        def up_proj(j):
            c0 = j * FF_CHUNK
            h2 = h2_ref[slot_f, rows]
            return (_dot(h2, wup_ref[0, :, c0:c0 + FF_CHUNK]),
                    _dot(h2, wup_ref[0, :, D_FF + c0:D_FF + c0 + FF_CHUNK]))

        def causal_conv(up, col0, slot, half):
            lanes = slice(half * FF_CHUNK, (half + 1) * FF_CHUNK)
            cols = slice(col0, col0 + FF_CHUNK)
            prev2, prev1 = [], []
            for g in range(seg):
                base = g * stage_rows
                up_g = up[seg_rows[g]]
                hist = cbuf_ref[g, :, cols]
                for k, prev in ((1, prev1), (2, prev2)):
                    ubuf_ref[slot, k - 1, base + k:base + k + CONV_HIST, lanes] = hist
                    ubuf_ref[slot, k - 1, base + k + CONV_HIST:base + k + CONV_HIST + sl, lanes] = up_g
                    prev.append(ubuf_ref[slot, k - 1, base + CONV_HIST:base + CONV_HIST + sl, lanes])
                cbuf_ref[g, :, cols] = up_g[sl - CONV_HIST:sl]
            if seg > 1:
                prev2, prev1 = [jnp.concatenate(prev2, axis=0)], [jnp.concatenate(prev1, axis=0)]
            w = wconv_ref[0, :, cols]
            return bconv_ref[0, :, cols] + w[0:1] * prev2[0] + w[1:2] * prev1[0] + w[2:3] * up

        ups = {j: up_proj(j) for j in range(FF_LOOKAHEAD)}
        for name in MIXER_STAGES_AFTER_CHUNK.get(-1, ()):
            mixer_stages[name]()
        yield
        for j in range(N_FF_CHUNKS):
            if j + FF_LOOKAHEAD < N_FF_CHUNKS:
                ups[j + FF_LOOKAHEAD] = up_proj(j + FF_LOOKAHEAD)
            up_val, up_gate = ups.pop(j)
            val = causal_conv(up_val, j * FF_CHUNK, 2 * r + j % 2, 0)
            gate = causal_conv(up_gate, D_FF + j * FF_CHUNK, 2 * r + j % 2, 1)
            act = ((gate * _sigmoid(gate)) * val).astype(BF16)
            down = _dot(act, wdown_ref[0, j * FF_CHUNK:(j + 1) * FF_CHUNK, :])
            if j == 0:
                acc_ref[r] = down
            else:
                acc_ref[r] += down
            for name in MIXER_STAGES_AFTER_CHUNK.get(j, ()):
                mixer_stages[name]()
            if j + 1 < N_FF_CHUNKS:
                yield
        x2 = x1_ref[slot_f, rows] + per_row(mod_rows(modf_ref, 5)) * acc_ref[r]
        if final:
            x2 = _rms(x2) * gfin_ref[...]
        y_ref[0, rows] = x2

    tiles = [tile_step(r) for r in range(unroll)]
    n_yields = N_FF_CHUNKS
    for r, tile in enumerate(tiles):
        done = TILE_OVERLAP_CHUNKS if r > 0 else 0
        for _ in range(n_yields - done):
            next(tile)
        if r + 1 < unroll:
            for _ in range(TILE_OVERLAP_CHUNKS):
                next(tiles[r + 1])
        for _ in tile:
            pass

    @pl.when((lax.rem(first_m + unroll - 1, nt) == nt - 1) & (s < n_blocks))
    def _store_mixer_state():
        for g in range(seg):
            poolo_ref[g] = hist_ref[g, POOL_HIST - POOL_BUF:POOL_HIST, :]
            co_ref[g] = st_ref[g, :, 0:HEAD_DIM, :]
            for hd in range(N_HEADS):
                no_ref[g, hd:hd + 1, :] = st_ref[g, hd, HEAD_DIM:HEAD_DIM + 1, :]
            mo_ref[g] = m_ref[g:g + 1, 0:N_HEADS]

    @pl.when((lax.rem(first_f + unroll - 1, nt) == nt - 1) & (s >= 1))
    def _store_conv_state():
        for g in range(seg):
            convo_ref[g] = cbuf_ref[g, CONV_HIST - CONV_BUF:CONV_HIST, :]


def _layer_weight(shape, layer):
    nd = len(shape)
    return pl.BlockSpec((1,) + tuple(shape), lambda s: (layer,) + (0,) * nd, pipeline_mode=pl.Buffered(1))


def _run_layer(x, mod, mod_row0, states, prm, g_final, layer, *, seg, sl, unroll, start, final):
    n_seq, t_len, _ = x.shape
    assert n_seq % seg == 0 and t_len % sl == 0 and mod_row0 % seg == 0 and (seg == 1 or t_len == sl)
    assert sl >= POOL_HIST and sl % SUBLANES == 0 and seg <= SUBLANES
    tt = seg * sl
    nt = t_len // sl
    n_groups = n_seq // seg
    assert nt % unroll == 0
    nb = nt // unroll
    n_blocks = n_groups * nb
    carried = states is not None
    xg = x.reshape(n_groups, seg * t_len, D_MODEL)
    mix_b = lambda s: jnp.minimum(s, n_blocks - 1) // nb
    mix_t = lambda s: jnp.minimum(s, n_blocks - 1) % nb
    ffn_b = lambda s: jnp.maximum(s - 1, 0) // nb
    ffn_t = lambda s: jnp.maximum(s - 1, 0) % nb
    mod_g0 = mod_row0 // seg
    in_specs = [
        pl.BlockSpec((1, unroll * tt, D_MODEL), lambda s: (mix_b(s), mix_t(s), 0)),
        pl.BlockSpec((1, seg, 6, D_MODEL), lambda s: (layer, mod_g0 + mix_b(s), 0, 0)),
        pl.BlockSpec((1, seg, 6, D_MODEL), lambda s: (layer, mod_g0 + ffn_b(s), 0, 0)),
    ]
    if carried:
        in_specs += [
            pl.BlockSpec((1, seg, POOL_BUF, D_POOL), lambda s: (layer, mix_b(s), 0, 0)),
            pl.BlockSpec((1, seg, N_HEADS, HEAD_DIM, HEAD_DIM), lambda s: (layer, mix_b(s), 0, 0, 0)),
            pl.BlockSpec((1, seg, N_HEADS, HEAD_DIM), lambda s: (layer, mix_b(s), 0, 0)),
            pl.BlockSpec((1, seg, 1, N_HEADS), lambda s: (layer, mix_b(s), 0, 0)),
            pl.BlockSpec((1, seg, CONV_BUF, 2 * D_FF), lambda s: (layer, ffn_b(s), 0, 0)),
        ]
    in_specs += [
        _layer_weight((1, D_MODEL), layer),
        _layer_weight((D_MODEL, D_INP), layer),
        _layer_weight((1, 2 * LANES), layer),
        _layer_weight((N_POOL_GROUPS, POOL_GROUP, POOL_GROUP), layer),
        _layer_weight((1, D_POOL), layer),
        _layer_weight((1, D_LSTM), layer),
        _layer_weight((D_MODEL, D_MODEL), layer),
        _layer_weight((1, D_MODEL), layer),
        _layer_weight((D_MODEL, 2 * D_FF), layer),
        _layer_weight((CONV_W, 2 * D_FF), layer),
        _layer_weight((1, 2 * D_FF), layer),
        _layer_weight((D_FF, D_MODEL), layer),
        pl.BlockSpec((1, D_MODEL), lambda s: (0, 0), pipeline_mode=pl.Buffered(1)),
    ]
    out_specs = [
        pl.BlockSpec((1, unroll * tt, D_MODEL), lambda s: (ffn_b(s), ffn_t(s), 0)),
        pl.BlockSpec((seg, POOL_BUF, D_POOL), lambda s: (mix_b(s), 0, 0)),
        pl.BlockSpec((seg, N_HEADS, HEAD_DIM, HEAD_DIM), lambda s: (mix_b(s), 0, 0, 0)),
        pl.BlockSpec((seg, N_HEADS, HEAD_DIM), lambda s: (mix_b(s), 0, 0)),
        pl.BlockSpec((seg, 1, N_HEADS), lambda s: (mix_b(s), 0, 0)),
        pl.BlockSpec((seg, CONV_BUF, 2 * D_FF), lambda s: (ffn_b(s), 0, 0)),
    ]
    out_shape = [
        jax.ShapeDtypeStruct((n_groups, seg * t_len, D_MODEL), F32),
        jax.ShapeDtypeStruct((n_seq, POOL_BUF, D_POOL), F32),
        jax.ShapeDtypeStruct((n_seq, N_HEADS, HEAD_DIM, HEAD_DIM), F32),
        jax.ShapeDtypeStruct((n_seq, N_HEADS, HEAD_DIM), F32),
        jax.ShapeDtypeStruct((n_seq, 1, N_HEADS), F32),
        jax.ShapeDtypeStruct((n_seq, CONV_BUF, 2 * D_FF), F32),
    ]
    scratch = [
        pltpu.VMEM((seg, POOL_HIST, D_POOL), F32),
        pltpu.VMEM((seg, N_HEADS, STATE_ROWS, HEAD_DIM), F32),
        pltpu.VMEM((SUBLANES, LANES), F32),
        pltpu.VMEM((seg, CONV_HIST, 2 * D_FF), F32),
        pltpu.VMEM((unroll, tt, D_MODEL), F32),
        pltpu.VMEM((2 * unroll, CONV_W - 1, seg * (2 * CONV_HIST + sl), 2 * FF_CHUNK), F32),
        pltpu.VMEM((2, unroll * tt, D_MODEL), F32),
        pltpu.VMEM((2, unroll * tt, D_MODEL), BF16),
    ]
    kern = functools.partial(_layer_kernel, seg=seg, sl=sl, nt=nt, unroll=unroll, n_blocks=n_blocks, start=start,
                             final=final, carried=carried)
    args = (xg, mod, mod) + (tuple(states) if carried else ()) + tuple(prm) + (g_final,)
    res = pl.pallas_call(
        kern,
        grid=(n_blocks + 1,),
        in_specs=in_specs,
        out_specs=out_specs,
        out_shape=out_shape,
        scratch_shapes=scratch,
        compiler_params=pltpu.CompilerParams(
            dimension_semantics=("arbitrary",),
            vmem_limit_bytes=VMEM_LIMIT_BYTES),
        name="layer_t%d" % tt,
    )(*args)
    return (res[0].reshape(n_seq, t_len, D_MODEL),) + tuple(res[1:])


def _prep_params(g_norm1, w_in, b_gate, w_pool, s_pool, g_head, w_out, g_norm2, w_up, w_conv, b_conv, w_down):
    depth = w_in.shape[0]
    lane_pad = ((0, 0), (0, 0), (0, LANES - N_HEADS))
    w_in_p = jnp.concatenate(
        [w_in[:, :, :D_MAIN], jnp.pad(w_in[:, :, D_MAIN:D_MAIN + N_HEADS], lane_pad),
         jnp.pad(w_in[:, :, D_MAIN + N_HEADS:], lane_pad)], axis=2).astype(BF16)
    bg3 = b_gate[:, None, :]
    bg = jnp.concatenate([jnp.pad(bg3[:, :, :N_HEADS], lane_pad), jnp.pad(bg3[:, :, N_HEADS:], lane_pad)], axis=2)
    return (g_norm1.reshape(depth, 1, D_MODEL), w_in_p, bg, w_pool.astype(BF16), s_pool.reshape(depth, 1, D_POOL),
            g_head.reshape(depth, 1, D_LSTM), w_out.astype(BF16), g_norm2.reshape(depth, 1, D_MODEL),
            w_up.astype(BF16), w_conv, b_conv.reshape(depth, 1, 2 * D_FF), w_down.astype(BF16))


def kernel(x_prompt, x_sample, state_pool, state_mlstm_C, state_mlstm_n, state_mlstm_m, state_conv, c_prompt, c_sample, w_ada, b_ada, g_norm1, w_in, b_gate, w_pool, s_pool, g_head, w_out, g_norm2, w_up, w_conv, b_conv, w_down, g_final):
    depth = w_ada.shape[0]
    bp, seq, _ = x_prompt.shape
    bs, dec_seq, _ = x_sample.shape
    tile_rows = MXU_DIM
    seg_s = max(1, min(tile_rows // dec_seq, SUBLANES))
    assert seq % tile_rows == 0 and bs % seg_s == 0 and bp % seg_s == 0

    mod = _adaln_mod(jnp.concatenate([c_prompt, c_sample], axis=0), w_ada, b_ada)
    mod = mod.reshape(depth, bp + bs, 6, D_MODEL)
    gfin = g_final.reshape(1, D_MODEL)
    prm = _prep_params(g_norm1, w_in, b_gate, w_pool, s_pool, g_head, w_out, g_norm2, w_up, w_conv, b_conv, w_down)
    sample_states = (state_pool, state_mlstm_C, state_mlstm_n,
                     state_mlstm_m.reshape(depth, bs, 1, N_HEADS), state_conv)

    xp, xs = x_prompt, x_sample
    outs_p, outs_s = [], []
    for l in range(depth):
        final = l == depth - 1
        res = _run_layer(xp, mod, 0, None, prm, gfin, l, seg=1, sl=tile_rows, unroll=PROMPT_TILES_PER_STEP, start=0,
                         final=final)
        xp = res[0]
        outs_p.append(res[1:])
        res = _run_layer(xs, mod, bp, sample_states, prm, gfin, l, seg=seg_s, sl=dec_seq, unroll=1, start=PAST_LEN,
                         final=final)
        xs = res[0]
        outs_s.append(res[1:])

    def stacked(outs):
        pool, c_mat, n_vec, m_vec, conv = (jnp.stack([o[i] for o in outs]) for i in range(5))
        return pool, c_mat, n_vec, m_vec.reshape(depth, -1, N_HEADS), conv

    return (xp, xs) + stacked(outs_p) + stacked(outs_s)
```
